```python
import jax, jax.numpy as jnp
from jax import lax
import numpy as np

D_MODEL = 4096
BATCH = 4
SEQ = 2048
DEPTH = 2
DEC_BATCH = 32
DEC_SEQ = 4
PAST_LEN = 16384
PAGE_SIZE = 128

N_A = DEPTH // 2
N_B = DEPTH - N_A
N_DENSE = (DEPTH + 1) // 2
N_MOE = DEPTH // 2
C_CONV = 3 * D_MODEL // 4
CONV_W = 31
HD = 128
N_Q = C_CONV // HD
N_KV = 8
GROUP = N_Q // N_KV
Q_W = N_Q * HD
KV_W = N_KV * HD
WINDOW = 128
MEM_TOKENS = 256
MEM_HEADS = 4
MEM_HD = D_MODEL // 16
MEM_W = MEM_HEADS * MEM_HD
D_FF = 7 * D_MODEL // 2
N_EXP = 8
TOP_K = 2
EPS = 1e-5
NEG_INF = -1e30

kernel_name = "yoco_conformer_swa_sink_moe_step"


def rms_norm(x, g):
    xf = x.astype(jnp.float32)
    y = xf * lax.rsqrt(jnp.mean(xf * xf, axis=-1, keepdims=True) + EPS)
    return (y * g.astype(jnp.float32)).astype(x.dtype)


def layer_norm(x, g, b):
    xf = x.astype(jnp.float32)
    mu = jnp.mean(xf, axis=-1, keepdims=True)
    var = jnp.mean(jnp.square(xf - mu), axis=-1, keepdims=True)
    y = (xf - mu) * lax.rsqrt(var + EPS)
    return (y * g.astype(jnp.float32) + b.astype(jnp.float32)).astype(x.dtype)


def swiglu(x, w_gu, w_down):
    gu = x @ w_gu
    g, u = gu[..., :D_FF], gu[..., D_FF:]
    return (jax.nn.silu(g) * u) @ w_down


def moe_swiglu(x, w_router, w_gu, w_down):
    logits = (x @ w_router).astype(jnp.float32)
    top_v, top_i = lax.top_k(logits, TOP_K)
    gates = jax.nn.softmax(top_v, axis=-1)
    gate_e = jnp.sum(jax.nn.one_hot(top_i, N_EXP, dtype=jnp.float32) * gates[..., None], axis=-2)
    out = jnp.zeros_like(x)
    for e in range(N_EXP):
        out = out + swiglu(x, w_gu[e], w_down[e]) * gate_e[..., e:e + 1].astype(x.dtype)
    return out


def memory_kv(mem, g, w):
    B, M, _ = mem.shape
    kv = rms_norm(mem, g) @ w
    k = kv[..., :MEM_W].reshape(B, M, MEM_HEADS, MEM_HD)
    v = kv[..., MEM_W:].reshape(B, M, MEM_HEADS, MEM_HD)
    return k, v


def memory_attention(q, k, v):
    B, T, _ = q.shape
    q = q.reshape(B, T, MEM_HEADS, MEM_HD)
    s = jnp.einsum("bthd,bmhd->bhtm", q, k).astype(jnp.float32) * (MEM_HD ** -0.5)
    p = jax.nn.softmax(s, axis=-1).astype(v.dtype)
    o = jnp.einsum("bhtm,bmhd->bthd", p, v)
    return o.reshape(B, T, MEM_W)


def conformer_conv(u, buf, w_dw, b_dw, ln_g, ln_b):
    a, gate = u[..., :C_CONV], u[..., C_CONV:]
    glu = a * jax.nn.sigmoid(gate)
    full = jnp.concatenate([buf, glu], axis=1)
    y = lax.conv_general_dilated(full, w_dw[:, None, :], (1,), "VALID",
                                 dimension_numbers=("NWC", "WIO", "NWC"),
                                 feature_group_count=C_CONV) + b_dw
    y = jax.nn.silu(layer_norm(y, ln_g, ln_b))
    return y, full[:, -(CONV_W - 1):]


def swa_sink_attention(q, k_sh, v_sh, k_past, v_past, sinks, pos0):
    B, T, _ = q.shape
    blk = T if T <= WINDOW else WINDOW
    n_blk = T // blk
    L = WINDOW + blk
    k_full = jnp.concatenate([k_past, k_sh], axis=1)
    v_full = jnp.concatenate([v_past, v_sh], axis=1)
    idx = jnp.arange(n_blk)[:, None] * blk + jnp.arange(L)[None, :]
    kb = k_full[:, idx]
    vb = v_full[:, idx]
    qb = q.reshape(B, n_blk, blk, N_KV, GROUP, HD)
    s = jnp.einsum("bnqhgd,bnkhd->bnhgqk", qb, kb).astype(jnp.float32) * (HD ** -0.5)
    rel = jnp.arange(blk)[:, None] + WINDOW - jnp.arange(L)[None, :]
    band = (rel >= 0) & (rel < WINDOW)
    key_pos = pos0 - WINDOW + idx
    allowed = band[None] & (key_pos >= 0)[:, None, :]
    s = jnp.where(allowed[None, :, None, None], s, NEG_INF)
    sink = jnp.broadcast_to(sinks.astype(jnp.float32).reshape(N_KV, GROUP)[:, :, None, None],
                            s.shape[:-1] + (1,))
    p = jax.nn.softmax(jnp.concatenate([s, sink], axis=-1), axis=-1)[..., :-1].astype(vb.dtype)
    o = jnp.einsum("bnhgqk,bnkhd->bnqhgd", p, vb)
    return o.reshape(B, T, Q_W)


def trunk(x, mem_k, mem_v, conv_buf, win_k_past, win_v_past, pos0, p):
    B, T, _ = x.shape
    h = x
    new_conv = []
    k_sh = v_sh = None
    win_k_new = win_v_new = None
    for l in range(DEPTH):
        hn = rms_norm(h, p["norm_mix"][l])
        if l < N_A:
            a = l
            z = hn @ p["w_in_a"][a]
            y_mix, buf = conformer_conv(z[..., :2 * C_CONV], conv_buf[a], p["w_dw"][a], p["b_dw"][a],
                                        p["ln_conv_g"][a], p["ln_conv_b"][a])
            new_conv.append(buf)
            q_mem = z[..., 2 * C_CONV:]
            w_out = p["w_out_a"][a]
        else:
            b = l - N_A
            z = hn @ p["w_in_b"][b]
            y_mix = swa_sink_attention(z[..., :Q_W], k_sh, v_sh, win_k_past, win_v_past, p["sinks"][b], pos0)
            q_mem = z[..., Q_W:]
            w_out = p["w_out_b"][b]
        y_mem = memory_attention(q_mem, mem_k[l], mem_v[l])
        h = h + jnp.concatenate([y_mix, y_mem], axis=-1) @ w_out
        hn = rms_norm(h, p["norm_ffn"][l])
        if l % 2 == 0:
            h = h + swiglu(hn, p["w_gu_dense"][l // 2], p["w_down_dense"][l // 2])
        else:
            h = h + moe_swiglu(hn, p["w_router"][l // 2], p["w_gu_exp"][l // 2], p["w_down_exp"][l // 2])
        if l == N_A - 1:
            kv = rms_norm(h, p["norm_kv"]) @ p["w_kv"]
            k_sh = kv[..., :KV_W].reshape(B, T, N_KV, HD)
            v_sh = kv[..., KV_W:].reshape(B, T, N_KV, HD)
            win_k_new = jnp.concatenate([win_k_past, k_sh], axis=1)[:, -WINDOW:]
            win_v_new = jnp.concatenate([win_v_past, v_sh], axis=1)[:, -WINDOW:]
    return rms_norm(h, p["norm_final"]), jnp.stack(new_conv, axis=0), win_k_new, win_v_new


def setup_inputs(seed: int = 0) -> dict:
    key = jax.random.key(seed)
    ks = iter(jax.random.split(key, 40))

    def nrm(shape, scale):
        return jax.random.normal(next(ks), shape, jnp.float32) * scale

    def gain(shape):
        return 1.0 + nrm(shape, 0.02)

    return {
        "x_prompt": nrm((BATCH, SEQ, D_MODEL), 1.0),
        "x_sample": nrm((DEC_BATCH, DEC_SEQ, D_MODEL), 1.0),
        "cache_mem_k": nrm((DEPTH, DEC_BATCH, MEM_TOKENS, MEM_HEADS, MEM_HD), 1.0),
        "cache_mem_v": nrm((DEPTH, DEC_BATCH, MEM_TOKENS, MEM_HEADS, MEM_HD), 1.0),
        "state_conv": nrm((N_A, DEC_BATCH, CONV_W - 1, C_CONV), 0.5),
        "cache_win_k": nrm((DEC_BATCH, WINDOW, N_KV, HD), 1.0),
        "cache_win_v": nrm((DEC_BATCH, WINDOW, N_KV, HD), 1.0),
        "mem_prompt": nrm((BATCH, MEM_TOKENS, D_MODEL), 1.0),
        "norm_mix": gain((DEPTH, D_MODEL)),
        "norm_ffn": gain((DEPTH, D_MODEL)),
        "norm_mem": gain((DEPTH, D_MODEL)),
        "norm_kv": gain((D_MODEL,)),
        "norm_final": gain((D_MODEL,)),
        "w_mem_kv": nrm((DEPTH, D_MODEL, 2 * MEM_W), D_MODEL ** -0.5),
        "w_in_a": nrm((N_A, D_MODEL, 2 * C_CONV + MEM_W), D_MODEL ** -0.5),
        "w_dw": nrm((N_A, CONV_W, C_CONV), CONV_W ** -0.5),
        "b_dw": nrm((N_A, C_CONV), 0.02),
        "ln_conv_g": gain((N_A, C_CONV)),
        "ln_conv_b": nrm((N_A, C_CONV), 0.02),
        "w_out_a": nrm((N_A, C_CONV + MEM_W, D_MODEL), (C_CONV + MEM_W) ** -0.5),
        "w_kv": nrm((D_MODEL, 2 * KV_W), D_MODEL ** -0.5),
        "w_in_b": nrm((N_B, D_MODEL, Q_W + MEM_W), D_MODEL ** -0.5),
        "sinks": nrm((N_B, N_Q), 0.5),
        "w_out_b": nrm((N_B, Q_W + MEM_W, D_MODEL), (Q_W + MEM_W) ** -0.5),
        "w_gu_dense": nrm((N_DENSE, D_MODEL, 2 * D_FF), D_MODEL ** -0.5),
        "w_down_dense": nrm((N_DENSE, D_FF, D_MODEL), D_FF ** -0.5),
        "w_router": nrm((N_MOE, D_MODEL, N_EXP), D_MODEL ** -0.5),
        "w_gu_exp": nrm((N_MOE, N_EXP, D_MODEL, 2 * D_FF), D_MODEL ** -0.5),
        "w_down_exp": nrm((N_MOE, N_EXP, D_FF, D_MODEL), D_FF ** -0.5),
    }


def reference(x_prompt, x_sample, cache_mem_k, cache_mem_v, state_conv, cache_win_k, cache_win_v, mem_prompt,
              norm_mix, norm_ffn, norm_mem, norm_kv, norm_final, w_mem_kv, w_in_a, w_dw, b_dw, ln_conv_g,
              ln_conv_b, w_out_a, w_kv, w_in_b, sinks, w_out_b, w_gu_dense, w_down_dense, w_router,
              w_gu_exp, w_down_exp):
    p = dict(norm_mix=norm_mix, norm_ffn=norm_ffn, norm_kv=norm_kv, norm_final=norm_final,
             w_in_a=w_in_a, w_dw=w_dw, b_dw=b_dw, ln_conv_g=ln_conv_g, ln_conv_b=ln_conv_b, w_out_a=w_out_a,
             w_kv=w_kv, w_in_b=w_in_b, sinks=sinks, w_out_b=w_out_b, w_gu_dense=w_gu_dense,
             w_down_dense=w_down_dense, w_router=w_router, w_gu_exp=w_gu_exp, w_down_exp=w_down_exp)
    mk, mv = [], []
    for l in range(DEPTH):
        k_l, v_l = memory_kv(mem_prompt, norm_mem[l], w_mem_kv[l])
        mk.append(k_l)
        mv.append(v_l)
    mem_k_p = jnp.stack(mk, axis=0)
    mem_v_p = jnp.stack(mv, axis=0)
    dt = x_prompt.dtype
    conv0 = jnp.zeros((N_A, BATCH, CONV_W - 1, C_CONV), dt)
    win0 = jnp.zeros((BATCH, WINDOW, N_KV, HD), dt)
    y_prompt, conv_p, wk_p, wv_p = trunk(x_prompt, mem_k_p, mem_v_p, conv0, win0, win0, 0, p)
    y_sample, conv_s, wk_s, wv_s = trunk(x_sample, cache_mem_k, cache_mem_v, state_conv, cache_win_k, cache_win_v,
                                         PAST_LEN, p)
    return (y_prompt, y_sample, mem_k_p, mem_v_p, conv_p, wk_p, wv_p, conv_s, wk_s, wv_s)
```

```python
import functools

import jax
import jax.numpy as jnp
from jax import lax
from jax.experimental import pallas as pl
from jax.experimental.pallas import tpu as pltpu

BF16 = jnp.bfloat16
F32 = jnp.float32
EPS = 1e-5
NEG_INF = -1e30
WINDOW = 128
HD = 128
MEM_HEADS = 4
CONV_W = 31
TOP_K = 2
PAST_LEN = 16384

V7X_VMEM_LIMIT_BYTES = 60 * 1024 * 1024
LANES = 128
BF16_SUBLANES = 16
HALO = 32


def _params(sem):
    return pltpu.CompilerParams(dimension_semantics=sem, vmem_limit_bytes=V7X_VMEM_LIMIT_BYTES)


def _pick_tile(n, target, mult):
    best = None
    for t in range(mult, min(n, target) + 1, mult):
        if n % t == 0:
            best = t
    assert best is not None, (n, target, mult)
    return best


def _addnorm_kernel(*refs, has_res, emit_sum, n_gain):
    it = iter(refs)
    h = next(it)[...]
    if has_res:
        h = h + next(it)[...]
    gains = [next(it) for _ in range(n_gain)]
    if emit_sum:
        next(it)[...] = h
    inv = lax.rsqrt(jnp.mean(h * h, axis=-1, keepdims=True) + EPS)
    for g in gains:
        o = next(it)
        o[...] = ((h * inv) * g[...]).astype(o.dtype)


def _addnorm(h, res, gains, *, emit_sum, out_dtype=BF16):
    M, D = h.shape
    tm = _pick_tile(M, 208, BF16_SUBLANES)
    row = pl.BlockSpec((tm, D), lambda i: (i, 0))
    gspec = pl.BlockSpec((1, D), lambda i: (0, 0))
    ins = [h] + ([res] if res is not None else []) + [g.reshape(1, D) for g in gains]
    in_specs = [row] + ([row] if res is not None else []) + [gspec] * len(gains)
    out_shape = ([jax.ShapeDtypeStruct((M, D), F32)] if emit_sum else []) + \
        [jax.ShapeDtypeStruct((M, D), out_dtype) for _ in gains]
    outs = pl.pallas_call(
        functools.partial(_addnorm_kernel, has_res=res is not None, emit_sum=emit_sum, n_gain=len(gains)),
        grid=(M // tm,), in_specs=in_specs, out_specs=[row] * len(out_shape), out_shape=out_shape,
        compiler_params=_params(("arbitrary",)), name="addnorm")(*ins)
    return outs


def _router_kernel(h_ref, y_ref, g_ref, wr_ref, hs_ref, o_ref, *, n_exp):
    h = h_ref[...] + y_ref[...]
    hs_ref[...] = h
    hn = (h * lax.rsqrt(jnp.mean(h * h, axis=-1, keepdims=True) + EPS)) * g_ref[...]
    logits = jnp.dot(hn.astype(BF16), wr_ref[...].astype(BF16), preferred_element_type=F32)
    lane = lax.broadcasted_iota(jnp.int32, logits.shape, 1)
    l1 = jnp.where(lane < n_exp, logits, -jnp.inf)
    m1 = jnp.max(l1, axis=-1, keepdims=True)
    i1 = jnp.min(jnp.where(l1 == m1, lane, LANES), axis=-1, keepdims=True)
    l2 = jnp.where(lane == i1, -jnp.inf, l1)
    m2 = jnp.max(l2, axis=-1, keepdims=True)
    i2 = jnp.min(jnp.where(l2 == m2, lane, LANES), axis=-1, keepdims=True)
    e = jnp.exp(m2 - m1)
    g1 = 1.0 / (1.0 + e)
    g2 = e / (1.0 + e)
    o_ref[...] = jnp.where(lane == 0, i1.astype(F32),
                           jnp.where(lane == 1, i2.astype(F32),
                                     jnp.where(lane == 2, g1, jnp.where(lane == 3, g2, 0.0))))


def _router(h, res, gain, w_router):
    M, D = h.shape
    n_exp = w_router.shape[1]
    wr = jnp.pad(w_router, ((0, 0), (0, LANES - n_exp)))
    tm = _pick_tile(M, 208, BF16_SUBLANES)
    row = pl.BlockSpec((tm, D), lambda i: (i, 0))
    return pl.pallas_call(
        functools.partial(_router_kernel, n_exp=n_exp),
        grid=(M // tm,),
        in_specs=[row, row, pl.BlockSpec((1, D), lambda i: (0, 0)), pl.BlockSpec((D, LANES), lambda i: (0, 0))],
        out_specs=[row, pl.BlockSpec((tm, LANES), lambda i: (i, 0))],
        out_shape=[jax.ShapeDtypeStruct((M, D), F32), jax.ShapeDtypeStruct((M, LANES), F32)],
        compiler_params=_params(("arbitrary",)), name="router")(h, res, gain.reshape(1, D), wr)


def _mm_kernel(x_ref, w_ref, o_ref):
    o_ref[...] = jnp.dot(x_ref[...], w_ref[...].astype(BF16), preferred_element_type=F32).astype(o_ref.dtype)


def _mm(x, w, *, bm, bn, out_dtype=F32):
    M, K = x.shape
    N = w.shape[1]
    return pl.pallas_call(
        _mm_kernel, grid=(M // bm, N // bn),
        in_specs=[pl.BlockSpec((bm, K), lambda i, j: (i, 0), pipeline_mode=pl.Buffered(1)),
                  pl.BlockSpec((K, bn), lambda i, j: (0, j))],
        out_specs=pl.BlockSpec((bm, bn), lambda i, j: (i, j)),
        out_shape=jax.ShapeDtypeStruct((M, N), out_dtype),
        compiler_params=_params(("arbitrary", "arbitrary")), name="mm")(x, w)


def _glu_up_kernel(te_ref, ns_ref, nv_ref, x_ref, wg_ref, wu_ref, o_ref, wgb_ref, wub_ref, *, bulk, gran, n_gran):
    ng = ns_ref[pl.program_id(0)]
    nb = bulk // gran

    def rows(r0, n):
        xs = x_ref[pl.ds(r0, n), :]
        g = jnp.dot(xs, wgb_ref[...], preferred_element_type=F32)
        u = jnp.dot(xs, wub_ref[...], preferred_element_type=F32)
        o_ref[pl.ds(r0, n), :] = ((g * jax.nn.sigmoid(g)) * u).astype(o_ref.dtype)

    @pl.when(ng > 0)
    def _():
        wgb_ref[...] = wg_ref[0].astype(BF16)
        wub_ref[...] = wu_ref[0].astype(BF16)

        @pl.when(ng >= nb)
        def _():
            rows(0, bulk)

        def body(s, c):
            rows(pl.multiple_of(s * gran, gran), gran)
            return c

        def zero(s, c):
            o_ref[pl.ds(pl.multiple_of(s * gran, gran), gran), :] = jnp.zeros((gran, o_ref.shape[1]), o_ref.dtype)
            return c

        if n_gran > nb:
            lax.fori_loop(jnp.where(ng >= nb, nb, 0), ng, body, 0)
            lax.fori_loop(ng, n_gran, zero, 0)


def _glu_up(x, w_gu, tile_e, tile_nsub, n_valid, *, bm, bulk, gran, bn):
    R, K = x.shape
    F = w_gu.shape[2] // 2
    T, J = R // bm, F // bn

    def tsel(t, nv):
        return jnp.minimum(t, nv[0] - 1)

    def jsel(t, j, nv):
        return jnp.where(t < nv[0], j, J - 1)

    gs = pltpu.PrefetchScalarGridSpec(
        num_scalar_prefetch=3, grid=(T, J),
        in_specs=[pl.BlockSpec((bm, K), lambda t, j, te, ns, nv: (tsel(t, nv), 0), pipeline_mode=pl.Buffered(1)),
                  pl.BlockSpec((1, K, bn), lambda t, j, te, ns, nv: (te[t], 0, jsel(t, j, nv))),
                  pl.BlockSpec((1, K, bn), lambda t, j, te, ns, nv: (te[t], 0, J + jsel(t, j, nv)))],
        out_specs=pl.BlockSpec((bm, bn), lambda t, j, te, ns, nv: (tsel(t, nv), jsel(t, j, nv))),
        scratch_shapes=[pltpu.VMEM((K, bn), BF16), pltpu.VMEM((K, bn), BF16)])
    return pl.pallas_call(
        functools.partial(_glu_up_kernel, bulk=bulk, gran=gran, n_gran=bm // gran), grid_spec=gs,
        out_shape=jax.ShapeDtypeStruct((R, F), BF16),
        compiler_params=_params(("arbitrary", "arbitrary")), name="glu_up")(tile_e, tile_nsub, n_valid, x, w_gu, w_gu)


def _down_kernel(te_ref, ns_ref, nv_ref, x_ref, w_ref, o_ref, wb_ref, *, bulk, gran, n_gran):
    ng = ns_ref[pl.program_id(0)]
    nb = bulk // gran
    k = pl.program_id(2)

    def rows(r0, n):
        o_ref[pl.ds(r0, n), :] += jnp.dot(x_ref[pl.ds(r0, n), :], wb_ref[...], preferred_element_type=F32)

    @pl.when(ng > 0)
    def _():
        wb_ref[...] = w_ref[0].astype(BF16)

        @pl.when(k == 0)
        def _():
            o_ref[...] = jnp.zeros(o_ref.shape, o_ref.dtype)

        @pl.when(ng >= nb)
        def _():
            rows(0, bulk)

        def body(s, c):
            rows(pl.multiple_of(s * gran, gran), gran)
            return c

        if n_gran > nb:
            lax.fori_loop(jnp.where(ng >= nb, nb, 0), ng, body, 0)


def _down(x, w, tile_e, tile_nsub, n_valid, *, bm, bulk, gran, bn, bk):
    R, F = x.shape
    N = w.shape[2]
    T, J, KK = R // bm, N // bn, F // bk

    def tsel(t, nv):
        return jnp.minimum(t, nv[0] - 1)

    def sel(t, a, last, nv):
        return jnp.where(t < nv[0], a, last)

    gs = pltpu.PrefetchScalarGridSpec(
        num_scalar_prefetch=3, grid=(T, J, KK),
        in_specs=[pl.BlockSpec((bm, bk), lambda t, j, k, te, ns, nv: (tsel(t, nv), sel(t, k, KK - 1, nv))),
                  pl.BlockSpec((1, bk, bn),
                               lambda t, j, k, te, ns, nv: (te[t], sel(t, k, KK - 1, nv), sel(t, j, J - 1, nv)))],
        out_specs=pl.BlockSpec((bm, bn), lambda t, j, k, te, ns, nv: (tsel(t, nv), sel(t, j, J - 1, nv))),
        scratch_shapes=[pltpu.VMEM((bk, bn), BF16)])
    return pl.pallas_call(
        functools.partial(_down_kernel, bulk=bulk, gran=gran, n_gran=bm // gran), grid_spec=gs,
        out_shape=jax.ShapeDtypeStruct((R, N), F32),
        compiler_params=_params(("arbitrary", "arbitrary", "arbitrary")), name="down")(
            tile_e, tile_nsub, n_valid, x, w)


def _gather_norm_kernel(valid_ref, idx_ref, h_hbm, g_ref, o_ref, buf, sem, *, sub):
    s = pl.program_id(0)

    @pl.when(valid_ref[s] > 0)
    def _():
        def start(r, c):
            pltpu.make_async_copy(h_hbm.at[idx_ref[0, 0, r]], buf.at[r], sem).start()
            return c

        lax.fori_loop(0, sub, start, 0, unroll=8)

        def wait(r, c):
            pltpu.make_async_copy(h_hbm.at[0], buf.at[0], sem).wait()
            return c

        lax.fori_loop(0, sub, wait, 0, unroll=8)
        h = buf[...]
        inv = lax.rsqrt(jnp.mean(h * h, axis=-1, keepdims=True) + EPS)
        o_ref[...] = ((h * inv) * g_ref[...]).astype(o_ref.dtype)

    @pl.when(valid_ref[s] == 0)
    def _():
        o_ref[...] = jnp.zeros(o_ref.shape, o_ref.dtype)


def _gather_norm(h, gain, src_idx, sub_valid, *, sub):
    M, D = h.shape
    S = sub_valid.shape[0]
    gs = pltpu.PrefetchScalarGridSpec(
        num_scalar_prefetch=1, grid=(S,),
        in_specs=[pl.BlockSpec((1, 1, sub), lambda s, v: (s, 0, 0), memory_space=pltpu.SMEM),
                  pl.BlockSpec(memory_space=pl.ANY),
                  pl.BlockSpec((1, D), lambda s, v: (0, 0))],
        out_specs=pl.BlockSpec((sub, D), lambda s, v: (s, 0)),
        scratch_shapes=[pltpu.VMEM((sub, D), F32), pltpu.SemaphoreType.DMA(())])
    return pl.pallas_call(
        functools.partial(_gather_norm_kernel, sub=sub), grid_spec=gs,
        out_shape=jax.ShapeDtypeStruct((S * sub, D), BF16),
        compiler_params=_params(("arbitrary",)), name="gather_norm")(
            sub_valid, src_idx.reshape(S, 1, sub), h, gain.reshape(1, D))


def _combine_kernel(p1_ref, p2_ref, h_ref, g1_ref, g2_ref, y_hbm, gain_ref, op_ref, os_ref, buf, sem, *, tm,
                    n_prompt_tiles):
    i = pl.program_id(0)

    def start(r, c):
        pltpu.make_async_copy(y_hbm.at[p1_ref[0, 0, r]], buf.at[0, r], sem).start()
        pltpu.make_async_copy(y_hbm.at[p2_ref[0, 0, r]], buf.at[1, r], sem).start()
        return c

    lax.fori_loop(0, tm, start, 0, unroll=8)

    def wait(r, c):
        pltpu.make_async_copy(y_hbm.at[0], buf.at[0, 0], sem).wait()
        pltpu.make_async_copy(y_hbm.at[0], buf.at[0, 0], sem).wait()
        return c

    lax.fori_loop(0, tm, wait, 0, unroll=8)
    h = h_ref[...] + (buf[0] * g1_ref[...] + buf[1] * g2_ref[...])
    y = ((h * lax.rsqrt(jnp.mean(h * h, axis=-1, keepdims=True) + EPS)) * gain_ref[...])

    @pl.when(i < n_prompt_tiles)
    def _():
        op_ref[...] = y

    @pl.when(i >= n_prompt_tiles)
    def _():
        os_ref[...] = y


def _combine(h, ys, pos1, pos2, g1, g2, gain, *, n_prompt, tm):
    M, D = h.shape
    n_tiles = M // tm
    npt = n_prompt // tm
    row = pl.BlockSpec((tm, D), lambda i: (i, 0))
    col = pl.BlockSpec((tm, 1), lambda i: (i, 0))
    idx = pl.BlockSpec((1, 1, tm), lambda i: (i, 0, 0), memory_space=pltpu.SMEM)
    return pl.pallas_call(
        functools.partial(_combine_kernel, tm=tm, n_prompt_tiles=npt), grid=(n_tiles,),
        in_specs=[idx, idx, row, col, col, pl.BlockSpec(memory_space=pl.ANY), pl.BlockSpec((1, D), lambda i: (0, 0))],
        out_specs=[pl.BlockSpec((tm, D), lambda i: (jnp.minimum(i, npt - 1), 0)),
                   pl.BlockSpec((tm, D), lambda i: (jnp.maximum(i - npt, 0), 0))],
        out_shape=[jax.ShapeDtypeStruct((n_prompt, D), F32), jax.ShapeDtypeStruct((M - n_prompt, D), F32)],
        scratch_shapes=[pltpu.VMEM((2, tm, D), F32), pltpu.SemaphoreType.DMA(())],
        compiler_params=_params(("arbitrary",)), name="combine")(
            pos1.reshape(n_tiles, 1, tm), pos2.reshape(n_tiles, 1, tm), h, g1, g2, ys, gain.reshape(1, D))


def _mem_attn_kernel(q_ref, k_ref, v_ref, cat_ref, o_ref, *, nseq, rows_per_seq, mem_tokens):
    del cat_ref
    R, W = q_ref.shape
    hd = W // MEM_HEADS
    k = k_ref[...].reshape(nseq * mem_tokens, W)
    v = v_ref[...].reshape(nseq * mem_tokens, W)
    q = q_ref[...]
    if nseq > 1:
        rs = lax.broadcasted_iota(jnp.int32, (R, nseq * mem_tokens), 0) // rows_per_seq
        cs = lax.broadcasted_iota(jnp.int32, (R, nseq * mem_tokens), 1) // mem_tokens
        same = rs == cs
    for hh in range(MEM_HEADS):
        sl = slice(hh * hd, (hh + 1) * hd)
        s = lax.dot_general(q[:, sl].astype(BF16), k[:, sl].astype(BF16), (((1,), (1,)), ((), ())),
                            preferred_element_type=F32) * (hd ** -0.5)
        if nseq > 1:
            s = jnp.where(same, s, NEG_INF)
        e = jnp.exp(s - jnp.max(s, axis=-1, keepdims=True))
        p = e / jnp.sum(e, axis=-1, keepdims=True)
        o = jnp.dot(p.astype(BF16), v[:, sl].astype(BF16), preferred_element_type=F32)
        o_ref[:, sl] = o.astype(o_ref.dtype)


def _mem_attn(z, qcol, k, v, kv_spec, cat, *, grid, row_map, nseq, rows_per_seq, tq, mem_tokens, W):
    ncol = cat.shape[1] // W - 1
    qspec = pl.BlockSpec((tq, W), lambda *g: (row_map(*g), qcol))
    ospec = pl.BlockSpec((tq, W), lambda *g: (row_map(*g), ncol))
    return pl.pallas_call(
        functools.partial(_mem_attn_kernel, nseq=nseq, rows_per_seq=rows_per_seq, mem_tokens=mem_tokens),
        grid=grid, in_specs=[qspec, kv_spec[0], kv_spec[1], pl.BlockSpec(memory_space=pl.ANY)], out_specs=ospec,
        out_shape=jax.ShapeDtypeStruct(cat.shape, cat.dtype), input_output_aliases={3: 0},
        compiler_params=_params(("arbitrary",) * len(grid)), name="mem_attn")(z, k, v, cat)


def _swa_kernel(sink_ref, q_ref, kp_ref, vp_ref, kc_ref, vc_ref, cat_ref, o_ref, *, nseq, tq, group, prev_from_grid,
                pos0):
    del cat_ref
    R = nseq * tq
    n_kv = kc_ref.shape[-1] // HD
    q = q_ref[...]
    kp = kp_ref[...].reshape(nseq * WINDOW, n_kv * HD)
    vp = vp_ref[...].reshape(nseq * WINDOW, n_kv * HD)
    kc = kc_ref[...]
    vc = vc_ref[...]
    GR = group * R
    rr = lax.broadcasted_iota(jnp.int32, (GR, nseq * WINDOW), 0) % R
    cp = lax.broadcasted_iota(jnp.int32, (GR, nseq * WINDOW), 1)
    prev_ok = ((rr // tq) == (cp // WINDOW)) & ((cp % WINDOW) > (rr % tq))
    if prev_from_grid:
        prev_ok = prev_ok & (pl.program_id(1) * tq + pos0 - WINDOW + (cp % WINDOW) >= 0)
    else:
        prev_ok = prev_ok & (pos0 - WINDOW + (cp % WINDOW) >= 0)
    rc = lax.broadcasted_iota(jnp.int32, (GR, R), 0) % R
    cc = lax.broadcasted_iota(jnp.int32, (GR, R), 1)
    cur_ok = ((rc // tq) == (cc // tq)) & ((cc % tq) <= (rc % tq))
    gidx = lax.broadcasted_iota(jnp.int32, (GR, 1), 0) // R
    dn = (((1,), (1,)), ((), ()))
    scale = HD ** -0.5
    for hh in range(n_kv):
        ksl = slice(hh * HD, (hh + 1) * HD)
        q3 = jnp.concatenate([q[:, (hh * group + g) * HD:(hh * group + g + 1) * HD] for g in range(group)],
                             axis=0).astype(BF16)
        sp = lax.dot_general(q3, kp[:, ksl].astype(BF16), dn, preferred_element_type=F32) * scale
        sc = lax.dot_general(q3, kc[:, ksl].astype(BF16), dn, preferred_element_type=F32) * scale
        sp = jnp.where(prev_ok, sp, NEG_INF)
        sc = jnp.where(cur_ok, sc, NEG_INF)
        sink = jnp.zeros((GR, 1), F32)
        for g in range(group):
            sink = jnp.where(gidx == g, sink_ref[hh * group + g], sink)
        m = jnp.maximum(jnp.maximum(jnp.max(sp, axis=-1, keepdims=True), jnp.max(sc, axis=-1, keepdims=True)), sink)
        ep = jnp.exp(sp - m)
        ec = jnp.exp(sc - m)
        den = jnp.sum(ep, axis=-1, keepdims=True) + jnp.sum(ec, axis=-1, keepdims=True) + jnp.exp(sink - m)
        o = jnp.dot((ep / den).astype(BF16), vp[:, ksl].astype(BF16), preferred_element_type=F32) + \
            jnp.dot((ec / den).astype(BF16), vc[:, ksl].astype(BF16), preferred_element_type=F32)
        for g in range(group):
            o_ref[:, (hh * group + g) * HD:(hh * group + g + 1) * HD] = o[g * R:(g + 1) * R].astype(o_ref.dtype)


def _swa(sinks, z, kprev, vprev, kvcur, cat, *, grid, row_map, prev_specs, nseq, tq, q_w, kv_w, prev_from_grid, pos0):
    R = nseq * tq
    group = q_w // kv_w
    in_specs = [pl.BlockSpec(memory_space=pltpu.SMEM),
                pl.BlockSpec((R, q_w), lambda *g: (row_map(*g), 0)),
                prev_specs[0], prev_specs[1],
                pl.BlockSpec((R, kv_w), lambda *g: (row_map(*g), 0)),
                pl.BlockSpec((R, kv_w), lambda *g: (row_map(*g), 1)),
                pl.BlockSpec(memory_space=pl.ANY)]
    return pl.pallas_call(
        functools.partial(_swa_kernel, nseq=nseq, tq=tq, group=group, prev_from_grid=prev_from_grid, pos0=pos0),
        grid=grid, in_specs=in_specs, out_specs=pl.BlockSpec((R, q_w), lambda *g: (row_map(*g), 0)),
        out_shape=jax.ShapeDtypeStruct(cat.shape, cat.dtype), input_output_aliases={6: 0},
        compiler_params=_params(("arbitrary",) * len(grid)), name="swa")(sinks, z, kprev, vprev, kvcur, kvcur, cat)


def _ln_silu(y, g, b):
    mu = jnp.mean(y, axis=-1, keepdims=True)
    d = y - mu
    var = jnp.mean(d * d, axis=-1, keepdims=True)
    yn = (d * lax.rsqrt(var + EPS)) * g + b
    return yn * jax.nn.sigmoid(yn)


def _conv_prompt_kernel(a_ref, gt_ref, ah_ref, gh_ref, w_ref, b_ref, lg_ref, lb_ref, o_ref, tail_ref, f_ref, y_ref,
                        *, tt):
    n = pl.program_id(1)
    C = a_ref.shape[1]
    glu = a_ref[...] * jax.nn.sigmoid(gt_ref[...])
    f_ref[HALO:HALO + tt, :] = glu.astype(BF16).astype(F32)
    halo = (ah_ref[...] * jax.nn.sigmoid(gh_ref[...])).astype(BF16).astype(F32)
    f_ref[0:HALO, :] = jnp.where(n > 0, halo, 0.0)

    @pl.when(n == pl.num_programs(1) - 1)
    def _():
        tail_ref[...] = glu[tt - HALO:tt, :]

    def chunk(c, carry):
        lanes = pl.ds(pl.multiple_of(c * LANES, LANES), LANES)
        acc = jnp.zeros((tt, LANES), F32)
        for b in range(8):
            rows = tt + (HALO if b == 0 else HALO - 8)
            sb = f_ref[b:b + rows, lanes]
            for a in range(5):
                w = 8 * a + b - (HALO - CONV_W + 1)
                if 0 <= w < CONV_W:
                    acc = acc + sb[8 * a:8 * a + tt, :] * w_ref[w:w + 1, lanes]
        y_ref[:, lanes] = acc + b_ref[:, lanes]
        return carry

    lax.fori_loop(0, C // LANES, chunk, 0)
    o_ref[...] = _ln_silu(y_ref[...], lg_ref[...], lb_ref[...]).astype(o_ref.dtype)


def _conv_prompt(z, w_dw, b_dw, ln_g, ln_b, *, batch, seq, C, cat_w, tt):
    M = z.shape[0]
    nt = seq // tt
    hb = tt // HALO
    wpad = jnp.pad(w_dw, ((0, HALO - CONV_W), (0, 0)))
    vec = pl.BlockSpec((1, C), lambda b, n: (0, 0))

    def halo_row(b, n):
        return jnp.maximum(b * (seq // HALO) + n * hb - 1, b * (seq // HALO))

    return pl.pallas_call(
        functools.partial(_conv_prompt_kernel, tt=tt), grid=(batch, nt),
        in_specs=[pl.BlockSpec((tt, C), lambda b, n: (b * nt + n, 0)),
                  pl.BlockSpec((tt, C), lambda b, n: (b * nt + n, 1)),
                  pl.BlockSpec((HALO, C), lambda b, n: (halo_row(b, n), 0)),
                  pl.BlockSpec((HALO, C), lambda b, n: (halo_row(b, n), 1)),
                  pl.BlockSpec((HALO, C), lambda b, n: (0, 0)), vec, vec, vec],
        out_specs=[pl.BlockSpec((tt, C), lambda b, n: (b * nt + n, 0)),
                   pl.BlockSpec((None, HALO, C), lambda b, n: (b, 0, 0))],
        out_shape=[jax.ShapeDtypeStruct((M, cat_w), BF16), jax.ShapeDtypeStruct((batch, HALO, C), F32)],
        scratch_shapes=[pltpu.VMEM((HALO + tt, C), F32), pltpu.VMEM((tt, C), F32)],
        compiler_params=_params(("arbitrary", "arbitrary")), name="conv_prompt")(
            z, z, z, z, wpad, b_dw.reshape(1, C), ln_g.reshape(1, C), ln_b.reshape(1, C))


def _conv_sample_kernel(a_ref, gt_ref, st_ref, wsh_ref, wnew_ref, b_ref, lg_ref, lb_ref, cat_ref, o_ref, glu_ref, *,
                        nseq, tq):
    del cat_ref
    R, C = a_ref.shape
    glu = a_ref[...] * jax.nn.sigmoid(gt_ref[...])
    glu_ref[...] = glu
    glu_b = glu.astype(BF16).astype(F32)
    row = lax.broadcasted_iota(jnp.int32, (R, C), 0)
    y = jnp.zeros((R, C), F32)
    for s in range(nseq):
        buf = st_ref[s].astype(BF16).astype(F32)
        for t in range(tq):
            yt = jnp.sum(buf * wsh_ref[t], axis=0, keepdims=True)
            for u in range(t + 1):
                yt = yt + glu_b[s * tq + u:s * tq + u + 1, :] * wnew_ref[t, u:u + 1, :]
            y = jnp.where(row == s * tq + t, yt, y)
    o_ref[...] = _ln_silu(y + b_ref[...], lg_ref[...], lb_ref[...]).astype(o_ref.dtype)


def _conv_sample(z, state, w_dw, b_dw, ln_g, ln_b, cat, *, n_prompt, dec_seq, C):
    nb, kw1, _ = state.shape
    nseq = 8 // dec_seq
    R = nseq * dec_seq
    base = n_prompt // R
    wsh = jnp.stack([jnp.concatenate([jnp.zeros((t, C), F32), w_dw[:kw1 - t]], axis=0) for t in range(dec_seq)])
    wnew = jnp.stack([jnp.stack([w_dw[kw1 - t + u] if u <= t else jnp.zeros((C,), F32) for u in range(dec_seq)])
                      for t in range(dec_seq)])
    vec = pl.BlockSpec((1, C), lambda s: (0, 0))
    return pl.pallas_call(
        functools.partial(_conv_sample_kernel, nseq=nseq, tq=dec_seq), grid=(nb // nseq,),
        in_specs=[pl.BlockSpec((R, C), lambda s: (base + s, 0)),
                  pl.BlockSpec((R, C), lambda s: (base + s, 1)),
                  pl.BlockSpec((nseq, kw1, C), lambda s: (s, 0, 0)),
                  pl.BlockSpec((dec_seq, kw1, C), lambda s: (0, 0, 0)),
                  pl.BlockSpec((dec_seq, dec_seq, C), lambda s: (0, 0, 0)),
                  vec, vec, vec, pl.BlockSpec(memory_space=pl.ANY)],
        out_specs=[pl.BlockSpec((R, C), lambda s: (base + s, 0)), pl.BlockSpec((R, C), lambda s: (s, 0))],
        out_shape=[jax.ShapeDtypeStruct(cat.shape, cat.dtype), jax.ShapeDtypeStruct((nb * dec_seq, C), F32)],
        input_output_aliases={8: 0},
        compiler_params=_params(("arbitrary",)), name="conv_sample")(
            z, z, state, wsh, wnew, b_dw.reshape(1, C), ln_g.reshape(1, C), ln_b.reshape(1, C), cat)


def _moe_tables(route, n_exp, tm, sub, n_tiles):
    M = route.shape[0]
    e1 = route[:, 0].astype(jnp.int32)
    e2 = route[:, 1].astype(jnp.int32)
    flat_e = jnp.concatenate([e1, e2])
    oh = (flat_e[:, None] == jnp.arange(n_exp, dtype=jnp.int32)[None, :]).astype(jnp.int32)
    csum = jnp.cumsum(oh, axis=0)
    rank = jnp.sum((csum - oh) * oh, axis=1)
    cnt = csum[-1]
    ntile = (cnt + tm - 1) // tm
    tend = jnp.cumsum(ntile)
    tstart = tend - ntile
    pos = jnp.sum(oh * tstart[None, :], axis=1) * tm + rank
    n_valid = tend[-1]
    tid = jnp.arange(n_tiles, dtype=jnp.int32)
    tile_e = jnp.minimum(jnp.sum((tid[:, None] >= tend[None, :]).astype(jnp.int32), axis=1), n_exp - 1)
    rows = jnp.clip(cnt[tile_e] - (tid - tstart[tile_e]) * tm, 0, tm)
    rows = jnp.where(tid < n_valid, rows, 0)
    tile_e = jnp.where(tid < n_valid, tile_e, tile_e[jnp.maximum(n_valid - 1, 0)])
    tile_nsub = (rows + sub - 1) // sub
    nsub_max = tm // sub
    sub_valid = (jnp.arange(nsub_max, dtype=jnp.int32)[None, :] < tile_nsub[:, None]).astype(jnp.int32).reshape(-1)
    tok = jnp.arange(M, dtype=jnp.int32)
    src = jnp.zeros((n_tiles * tm,), jnp.int32).at[pos].set(jnp.concatenate([tok, tok]))
    return (pos[:M], pos[M:], src, sub_valid, tile_e.astype(jnp.int32), tile_nsub.astype(jnp.int32),
            n_valid.reshape(1).astype(jnp.int32))


def kernel(x_prompt, x_sample, cache_mem_k, cache_mem_v, state_conv, cache_win_k, cache_win_v, mem_prompt, norm_mix,
           norm_ffn, norm_mem, norm_kv, norm_final, w_mem_kv, w_in_a, w_dw, b_dw, ln_conv_g, ln_conv_b, w_out_a, w_kv,
           w_in_b, sinks, w_out_b, w_gu_dense, w_down_dense, w_router, w_gu_exp, w_down_exp):
    B, T, D = x_prompt.shape
    DB, DT, _ = x_sample.shape
    depth, _, MT, MH, MHD = cache_mem_k.shape
    C = state_conv.shape[-1]
    n_kv = cache_win_k.shape[2]
    KVW = n_kv * HD
    MW = MH * MHD
    n_exp = w_router.shape[-1]
    F = w_down_dense.shape[1]
    NP, NS = B * T, DB * DT
    M = NP + NS
    assert depth == 2 and MH == MEM_HEADS and C % MW == 0 and (8 % DT) == 0 and T % WINDOW == 0
    seqs_per_blk = 8 // DT
    sblk = seqs_per_blk * DT
    sbase = NP // sblk

    bm = _pick_tile(M, 2080, BF16_SUBLANES)
    bn = _pick_tile(MW, 512, LANES)

    mem2d = mem_prompt.reshape(B * MT, D)
    mem_kv = []
    for l in range(depth):
        (mn,) = _addnorm(mem2d, None, [norm_mem[l]], emit_sum=False)
        mem_kv.append(_mm(mn, w_mem_kv[l], bm=B * MT, bn=bn))
    mem_k_p = jnp.stack([kv[:, :MW].reshape(B, MT, MH, MHD) for kv in mem_kv])
    mem_v_p = jnp.stack([kv[:, MW:].reshape(B, MT, MH, MHD) for kv in mem_kv])
    cmk = cache_mem_k.reshape(depth, DB, MT, MW)
    cmv = cache_mem_v.reshape(depth, DB, MT, MW)

    tq_mem = _pick_tile(T, 512, 8)

    def mem_attention(z, qcol, l, cat):
        cat = _mem_attn(z, qcol, mem_kv[l], mem_kv[l],
                        (pl.BlockSpec((MT, MW), lambda b, n: (b, 0)), pl.BlockSpec((MT, MW), lambda b, n: (b, 1))),
                        cat, grid=(B, T // tq_mem), row_map=lambda b, n: b * (T // tq_mem) + n, nseq=1,
                        rows_per_seq=tq_mem, tq=tq_mem, mem_tokens=MT, W=MW)
        sspec = pl.BlockSpec((None, seqs_per_blk, MT, MW), lambda s: (l, s, 0, 0))
        return _mem_attn(z, qcol, cmk, cmv, (sspec, sspec), cat, grid=(DB // seqs_per_blk,),
                         row_map=lambda s: sbase + s, nseq=seqs_per_blk, rows_per_seq=DT, tq=sblk, mem_tokens=MT, W=MW)

    h0 = jnp.concatenate([x_prompt.reshape(NP, D), x_sample.reshape(NS, D)], axis=0)

    (hn,) = _addnorm(h0, None, [norm_mix[0]], emit_sum=False)
    z1 = _mm(hn, w_in_a[0], bm=bm, bn=bn)
    cat, tail = _conv_prompt(z1, w_dw[0], b_dw[0], ln_conv_g[0], ln_conv_b[0], batch=B, seq=T, C=C, cat_w=C + MW,
                             tt=WINDOW)
    cat, glu_s = _conv_sample(z1, state_conv[0], w_dw[0], b_dw[0], ln_conv_g[0], ln_conv_b[0], cat, n_prompt=NP,
                              dec_seq=DT, C=C)
    cat = mem_attention(z1, 2 * C // MW, 0, cat)
    y = _mm(cat, w_out_a[0], bm=bm, bn=bn)
    h1, hn = _addnorm(h0, y, [norm_ffn[0]], emit_sum=True)
    one = jnp.ones((M // bm,), jnp.int32)
    dense_tabs = (jnp.zeros((M // bm,), jnp.int32), one, jnp.full((1,), M // bm, jnp.int32))
    act = _glu_up(hn, w_gu_dense, *dense_tabs, bm=bm, bulk=bm, gran=bm, bn=_pick_tile(F, 256, LANES))
    y = _down(act, w_down_dense, *dense_tabs, bm=bm, bulk=bm, gran=bm, bn=_pick_tile(D, 1024, LANES),
              bk=_pick_tile(F, 1024, LANES))
    h2, hn_kv, hn = _addnorm(h1, y, [norm_kv, norm_mix[1]], emit_sum=True)

    kv = _mm(hn_kv, w_kv, bm=bm, bn=bn)
    k_new, v_new = kv[:, :KVW], kv[:, KVW:]

    z2 = _mm(hn, w_in_b[0], bm=bm, bn=bn)
    nblk = T // WINDOW
    cat = jnp.zeros((M, C + MW), BF16)

    def prev_row(b, n):
        return jnp.maximum(b * nblk + n - 1, b * nblk)

    cat = _swa(sinks[0], z2, kv, kv, kv, cat, grid=(B, nblk), row_map=lambda b, n: b * nblk + n,
               prev_specs=(pl.BlockSpec((WINDOW, KVW), lambda b, n: (prev_row(b, n), 0)),
                           pl.BlockSpec((WINDOW, KVW), lambda b, n: (prev_row(b, n), 1))),
               nseq=1, tq=WINDOW, q_w=C, kv_w=KVW, prev_from_grid=True, pos0=0)
    pspec = pl.BlockSpec((seqs_per_blk, WINDOW, KVW), lambda s: (s, 0, 0))
    cat = _swa(sinks[0], z2, cache_win_k.reshape(DB, WINDOW, KVW), cache_win_v.reshape(DB, WINDOW, KVW), kv, cat,
               grid=(DB // seqs_per_blk,), row_map=lambda s: sbase + s, prev_specs=(pspec, pspec),
               nseq=seqs_per_blk, tq=DT, q_w=C, kv_w=KVW, prev_from_grid=False, pos0=PAST_LEN)
    cat = mem_attention(z2, C // MW, 1, cat)
    y = _mm(cat, w_out_b[0], bm=bm, bn=bn)
    h3, route = _router(h2, y, norm_ffn[1], w_router[0])

    mean_rows = TOP_K * M // n_exp
    gran = 128 if mean_rows >= 1024 else 16
    tm = -(-(mean_rows * 6 // 5) // (2 * gran)) * (2 * gran)
    bulk = (mean_rows * 12 // 13) // gran * gran
    n_tiles = TOP_K * M // tm + n_exp
    pos1, pos2, src, sub_valid, tile_e, tile_ng, n_valid = _moe_tables(route, n_exp, tm, gran, n_tiles)
    xs = _gather_norm(h3, norm_ffn[1], src, sub_valid, sub=gran)
    act = _glu_up(xs, w_gu_exp[0], tile_e, tile_ng, n_valid, bm=tm, bulk=bulk, gran=gran,
                  bn=_pick_tile(F, 256, LANES))
    ys = _down(act, w_down_exp[0], tile_e, tile_ng, n_valid, bm=tm, bulk=bulk, gran=gran,
               bn=_pick_tile(D, 1024, LANES), bk=_pick_tile(F, 1024, LANES))
    y_p, y_s = _combine(h3, ys, pos1, pos2, route[:, 2:3], route[:, 3:4], norm_final, n_prompt=NP,
                        tm=_pick_tile(NS, 128, 8))

    conv_p = tail[None, :, HALO - (CONV_W - 1):, :]
    conv_s = jnp.concatenate([state_conv[0][:, DT:], glu_s.reshape(DB, DT, C)], axis=1)[None]
    kp = k_new[:NP].reshape(B, T, n_kv, HD)[:, T - WINDOW:]
    vp = v_new[:NP].reshape(B, T, n_kv, HD)[:, T - WINDOW:]
    ks = jnp.concatenate([cache_win_k, k_new[NP:].reshape(DB, DT, n_kv, HD)], axis=1)[:, -WINDOW:]
    vs = jnp.concatenate([cache_win_v, v_new[NP:].reshape(DB, DT, n_kv, HD)], axis=1)[:, -WINDOW:]
    return (y_p.reshape(B, T, D), y_s.reshape(DB, DT, D), mem_k_p, mem_v_p, conv_p, kp, vp, conv_s, ks, vs)
```

```python
import functools

import jax
import jax.numpy as jnp
from jax import lax
from jax.experimental import pallas as pl
from jax.experimental.pallas import tpu as pltpu

BF16 = jnp.bfloat16
F32 = jnp.float32
EPS = 1e-5
NEG_INF = -1e30
WINDOW = 128
HD = 128
MEM_HEADS = 4
CONV_W = 31
TOP_K = 2
PAST_LEN = 16384

V7X_VMEM_LIMIT_BYTES = 60 * 1024 * 1024
LANES = 128
BF16_SUBLANES = 16
HALO = 32


def _params(sem):
    return pltpu.CompilerParams(dimension_semantics=sem, vmem_limit_bytes=V7X_VMEM_LIMIT_BYTES)


def _pick_tile(n, target, mult):
    best = None
    for t in range(mult, min(n, target) + 1, mult):
        if n % t == 0:
            best = t
    assert best is not None, (n, target, mult)
    return best


def _addnorm_kernel(*refs, has_res, emit_sum, n_gain):
    it = iter(refs)
    h = next(it)[...]
    if has_res:
        h = h + next(it)[...]
    gains = [next(it) for _ in range(n_gain)]
    if emit_sum:
        next(it)[...] = h
    inv = lax.rsqrt(jnp.mean(h * h, axis=-1, keepdims=True) + EPS)
    for g in gains:
        o = next(it)
        o[...] = ((h * inv) * g[...]).astype(o.dtype)


def _addnorm(h, res, gains, *, emit_sum, out_dtype=BF16):
    M, D = h.shape
    tm = _pick_tile(M, 208, BF16_SUBLANES)
    row = pl.BlockSpec((tm, D), lambda i: (i, 0))
    gspec = pl.BlockSpec((1, D), lambda i: (0, 0))
    ins = [h] + ([res] if res is not None else []) + [g.reshape(1, D) for g in gains]
    in_specs = [row] + ([row] if res is not None else []) + [gspec] * len(gains)
    out_shape = ([jax.ShapeDtypeStruct((M, D), F32)] if emit_sum else []) + \
        [jax.ShapeDtypeStruct((M, D), out_dtype) for _ in gains]
    outs = pl.pallas_call(
        functools.partial(_addnorm_kernel, has_res=res is not None, emit_sum=emit_sum, n_gain=len(gains)),
        grid=(M // tm,), in_specs=in_specs, out_specs=[row] * len(out_shape), out_shape=out_shape,
        compiler_params=_params(("arbitrary",)), name="addnorm")(*ins)
    return outs


def _router_kernel(h_ref, y_ref, g_ref, wr_ref, hs_ref, o_ref, *, n_exp):
    h = h_ref[...] + y_ref[...]
    hs_ref[...] = h
    hn = (h * lax.rsqrt(jnp.mean(h * h, axis=-1, keepdims=True) + EPS)) * g_ref[...]
    logits = jnp.dot(hn.astype(BF16), wr_ref[...].astype(BF16), preferred_element_type=F32)
    lane = lax.broadcasted_iota(jnp.int32, logits.shape, 1)
    l1 = jnp.where(lane < n_exp, logits, -jnp.inf)
    m1 = jnp.max(l1, axis=-1, keepdims=True)
    i1 = jnp.min(jnp.where(l1 == m1, lane, LANES), axis=-1, keepdims=True)
    l2 = jnp.where(lane == i1, -jnp.inf, l1)
    m2 = jnp.max(l2, axis=-1, keepdims=True)
    i2 = jnp.min(jnp.where(l2 == m2, lane, LANES), axis=-1, keepdims=True)
    e = jnp.exp(m2 - m1)
    g1 = 1.0 / (1.0 + e)
    g2 = e / (1.0 + e)
    o_ref[...] = jnp.where(lane == 0, i1.astype(F32),
                           jnp.where(lane == 1, i2.astype(F32),
                                     jnp.where(lane == 2, g1, jnp.where(lane == 3, g2, 0.0))))


def _router(h, res, gain, w_router):
    M, D = h.shape
    n_exp = w_router.shape[1]
    wr = jnp.pad(w_router, ((0, 0), (0, LANES - n_exp)))
    tm = _pick_tile(M, 208, BF16_SUBLANES)
    row = pl.BlockSpec((tm, D), lambda i: (i, 0))
    return pl.pallas_call(
        functools.partial(_router_kernel, n_exp=n_exp),
        grid=(M // tm,),
        in_specs=[row, row, pl.BlockSpec((1, D), lambda i: (0, 0)), pl.BlockSpec((D, LANES), lambda i: (0, 0))],
        out_specs=[row, pl.BlockSpec((tm, LANES), lambda i: (i, 0))],
        out_shape=[jax.ShapeDtypeStruct((M, D), F32), jax.ShapeDtypeStruct((M, LANES), F32)],
        compiler_params=_params(("arbitrary",)), name="router")(h, res, gain.reshape(1, D), wr)


W_STREAMS = 4


def _cast_bands(w_refs, wb_ref):
    kb = wb_ref.shape[0] // len(w_refs)
    for q, w_ref in enumerate(w_refs):
        wb_ref[q * kb:(q + 1) * kb, :] = w_ref[...].reshape(kb, wb_ref.shape[1]).astype(BF16)


def _mm_kernel(x_ref, *rest):
    *w_refs, o_ref, wb_ref = rest
    _cast_bands(w_refs, wb_ref)
    o_ref[...] = jnp.dot(x_ref[...], wb_ref[...], preferred_element_type=F32).astype(o_ref.dtype)


def _mm(x, w, *, bm, bn, out_dtype=F32):
    M, K = x.shape
    N = w.shape[1]
    kb = K // W_STREAMS
    return pl.pallas_call(
        _mm_kernel, grid=(M // bm, N // bn),
        in_specs=[pl.BlockSpec((bm, K), lambda i, j: (i, 0), pipeline_mode=pl.Buffered(1))] +
                 [pl.BlockSpec((kb, bn), functools.partial(lambda i, j, q: (q, j), q=q)) for q in range(W_STREAMS)],
        out_specs=pl.BlockSpec((bm, bn), lambda i, j: (i, j)),
        out_shape=jax.ShapeDtypeStruct((M, N), out_dtype),
        scratch_shapes=[pltpu.VMEM((K, bn), BF16)],
        compiler_params=_params(("arbitrary", "arbitrary")), name="mm")(x, *([w] * W_STREAMS))


def _row_loops(need, chunk, gran, fn):
    total = (need + gran - 1) // gran * gran
    nc = total // chunk

    def big(c, carry):
        fn(pl.multiple_of(c * chunk, chunk), chunk)
        return carry

    lax.fori_loop(0, nc, big, 0)
    base = nc * chunk
    size = chunk // 2
    while size >= gran:
        take = (total - base) >= size
        pl.when(take)(functools.partial(fn, pl.multiple_of(base, size), size))
        base = base + jnp.where(take, size, 0)
        size //= 2
    return total


def _glu_up_kernel(te_ref, ns_ref, nv_ref, x_ref, *rest, chunk, gran):
    *w_refs, o_ref, wgb_ref, wub_ref = rest
    half = len(w_refs) // 2
    need = ns_ref[pl.program_id(0)]
    bm = o_ref.shape[0]

    def rows(r0, n):
        xs = x_ref[pl.ds(r0, n), :]
        g = jnp.dot(xs, wgb_ref[...], preferred_element_type=F32)
        u = jnp.dot(xs, wub_ref[...], preferred_element_type=F32)
        o_ref[pl.ds(r0, n), :] = ((g * jax.nn.sigmoid(g)) * u).astype(o_ref.dtype)

    @pl.when(need > 0)
    def _():
        _cast_bands(w_refs[:half], wgb_ref)
        _cast_bands(w_refs[half:], wub_ref)
        done = _row_loops(need, chunk, gran, rows)

        def zero(g, carry):
            o_ref[pl.ds(pl.multiple_of(done + g * gran, gran), gran), :] = jnp.zeros((gran, o_ref.shape[1]),
                                                                                     o_ref.dtype)
            return carry

        if gran < bm:
            lax.fori_loop(0, (bm - done) // gran, zero, 0)


def _glu_up(x, w_gu, tile_e, tile_rows, n_valid, *, bm, chunk, gran, bn):
    R, K = x.shape
    F = w_gu.shape[2] // 2
    T, J = R // bm, F // bn
    nq = W_STREAMS // 2

    def tsel(t, nv):
        return jnp.minimum(t, nv[0] - 1)

    def jsel(t, j, nv):
        return jnp.where(t < nv[0], j, J - 1)

    gs = pltpu.PrefetchScalarGridSpec(
        num_scalar_prefetch=3, grid=(T, J),
        in_specs=[pl.BlockSpec((bm, K), lambda t, j, te, ns, nv: (tsel(t, nv), 0), pipeline_mode=pl.Buffered(1))] +
                 [pl.BlockSpec((1, K // nq, bn), functools.partial(
                     lambda t, j, te, ns, nv, q, off: (te[t], q, off + jsel(t, j, nv)), q=q, off=off))
                  for off in (0, J) for q in range(nq)],
        out_specs=pl.BlockSpec((bm, bn), lambda t, j, te, ns, nv: (tsel(t, nv), jsel(t, j, nv))),
        scratch_shapes=[pltpu.VMEM((K, bn), BF16), pltpu.VMEM((K, bn), BF16)])
    return pl.pallas_call(
        functools.partial(_glu_up_kernel, chunk=chunk, gran=gran), grid_spec=gs,
        out_shape=jax.ShapeDtypeStruct((R, F), BF16),
        compiler_params=_params(("arbitrary", "arbitrary")), name="glu_up")(
            tile_e, tile_rows, n_valid, x, *([w_gu] * (2 * nq)))


def _down_kernel(te_ref, ns_ref, nv_ref, x_ref, w_ref, o_ref, wb_ref, *, chunk, gran):
    need = ns_ref[pl.program_id(0)]
    k = pl.program_id(2)

    def rows(r0, n):
        o_ref[pl.ds(r0, n), :] += jnp.dot(x_ref[pl.ds(r0, n), :], wb_ref[...], preferred_element_type=F32)

    @pl.when(need > 0)
    def _():
        wb_ref[...] = w_ref[0].astype(BF16)

        @pl.when(k == 0)
        def _():
            o_ref[...] = jnp.zeros(o_ref.shape, o_ref.dtype)

        _row_loops(need, chunk, gran, rows)


def _down(x, w, tile_e, tile_rows, n_valid, *, bm, chunk, gran, bn, bk):
    R, F = x.shape
    N = w.shape[2]
    T, J, KK = R // bm, N // bn, F // bk

    def tsel(t, nv):
        return jnp.minimum(t, nv[0] - 1)

    def sel(t, a, last, nv):
        return jnp.where(t < nv[0], a, last)

    gs = pltpu.PrefetchScalarGridSpec(
        num_scalar_prefetch=3, grid=(T, J, KK),
        in_specs=[pl.BlockSpec((bm, bk), lambda t, j, k, te, ns, nv: (tsel(t, nv), sel(t, k, KK - 1, nv))),
                  pl.BlockSpec((1, bk, bn),
                               lambda t, j, k, te, ns, nv: (te[t], sel(t, k, KK - 1, nv), sel(t, j, J - 1, nv)))],
        out_specs=pl.BlockSpec((bm, bn), lambda t, j, k, te, ns, nv: (tsel(t, nv), sel(t, j, J - 1, nv))),
        scratch_shapes=[pltpu.VMEM((bk, bn), BF16)])
    return pl.pallas_call(
        functools.partial(_down_kernel, chunk=chunk, gran=gran), grid_spec=gs,
        out_shape=jax.ShapeDtypeStruct((R, N), F32),
        compiler_params=_params(("arbitrary", "arbitrary", "arbitrary")), name="down")(
            tile_e, tile_rows, n_valid, x, w)


def _gather_norm_kernel(valid_ref, idx_ref, h_hbm, g_ref, o_ref, buf, sem, *, sub):
    s = pl.program_id(0)

    @pl.when(valid_ref[s] > 0)
    def _():
        def start(r, c):
            pltpu.make_async_copy(h_hbm.at[idx_ref[0, 0, r]], buf.at[r], sem).start()
            return c

        lax.fori_loop(0, sub, start, 0, unroll=8)

        def wait(r, c):
            pltpu.make_async_copy(h_hbm.at[0], buf.at[0], sem).wait()
            return c

        lax.fori_loop(0, sub, wait, 0, unroll=8)
        h = buf[...]
        inv = lax.rsqrt(jnp.mean(h * h, axis=-1, keepdims=True) + EPS)
        o_ref[...] = ((h * inv) * g_ref[...]).astype(o_ref.dtype)

    @pl.when(valid_ref[s] == 0)
    def _():
        o_ref[...] = jnp.zeros(o_ref.shape, o_ref.dtype)


def _gather_norm(h, gain, src_idx, sub_valid, *, sub):
    M, D = h.shape
    S = sub_valid.shape[0]
    gs = pltpu.PrefetchScalarGridSpec(
        num_scalar_prefetch=1, grid=(S,),
        in_specs=[pl.BlockSpec((1, 1, sub), lambda s, v: (s, 0, 0), memory_space=pltpu.SMEM),
                  pl.BlockSpec(memory_space=pl.ANY),
                  pl.BlockSpec((1, D), lambda s, v: (0, 0))],
        out_specs=pl.BlockSpec((sub, D), lambda s, v: (s, 0)),
        scratch_shapes=[pltpu.VMEM((sub, D), F32), pltpu.SemaphoreType.DMA(())])
    return pl.pallas_call(
        functools.partial(_gather_norm_kernel, sub=sub), grid_spec=gs,
        out_shape=jax.ShapeDtypeStruct((S * sub, D), BF16),
        compiler_params=_params(("arbitrary",)), name="gather_norm")(
            sub_valid, src_idx.reshape(S, 1, sub), h, gain.reshape(1, D))


def _combine_kernel(p1_ref, p2_ref, h_ref, g1_ref, g2_ref, y_hbm, gain_ref, op_ref, os_ref, buf, sem, *, tm,
                    n_prompt_tiles):
    i = pl.program_id(0)

    def start(r, c):
        pltpu.make_async_copy(y_hbm.at[p1_ref[0, 0, r]], buf.at[0, r], sem).start()
        pltpu.make_async_copy(y_hbm.at[p2_ref[0, 0, r]], buf.at[1, r], sem).start()
        return c

    lax.fori_loop(0, tm, start, 0, unroll=8)

    def wait(r, c):
        pltpu.make_async_copy(y_hbm.at[0], buf.at[0, 0], sem).wait()
        pltpu.make_async_copy(y_hbm.at[0], buf.at[0, 0], sem).wait()
        return c

    lax.fori_loop(0, tm, wait, 0, unroll=8)
    h = h_ref[...] + (buf[0] * g1_ref[...] + buf[1] * g2_ref[...])
    y = ((h * lax.rsqrt(jnp.mean(h * h, axis=-1, keepdims=True) + EPS)) * gain_ref[...])

    @pl.when(i < n_prompt_tiles)
    def _():
        op_ref[...] = y

    @pl.when(i >= n_prompt_tiles)
    def _():
        os_ref[...] = y


def _combine(h, ys, pos1, pos2, g1, g2, gain, *, n_prompt, tm):
    M, D = h.shape
    n_tiles = M // tm
    npt = n_prompt // tm
    row = pl.BlockSpec((tm, D), lambda i: (i, 0))
    col = pl.BlockSpec((tm, 1), lambda i: (i, 0))
    idx = pl.BlockSpec((1, 1, tm), lambda i: (i, 0, 0), memory_space=pltpu.SMEM)
    return pl.pallas_call(
        functools.partial(_combine_kernel, tm=tm, n_prompt_tiles=npt), grid=(n_tiles,),
        in_specs=[idx, idx, row, col, col, pl.BlockSpec(memory_space=pl.ANY), pl.BlockSpec((1, D), lambda i: (0, 0))],
        out_specs=[pl.BlockSpec((tm, D), lambda i: (jnp.minimum(i, npt - 1), 0)),
                   pl.BlockSpec((tm, D), lambda i: (jnp.maximum(i - npt, 0), 0))],
        out_shape=[jax.ShapeDtypeStruct((n_prompt, D), F32), jax.ShapeDtypeStruct((M - n_prompt, D), F32)],
        scratch_shapes=[pltpu.VMEM((2, tm, D), F32), pltpu.SemaphoreType.DMA(())],
        compiler_params=_params(("arbitrary",)), name="combine")(
            pos1.reshape(n_tiles, 1, tm), pos2.reshape(n_tiles, 1, tm), h, g1, g2, ys, gain.reshape(1, D))


def _mem_attn_kernel(q_ref, k_ref, v_ref, cat_ref, o_ref, *, nseq, rows_per_seq, mem_tokens):
    del cat_ref
    R, W = q_ref.shape
    hd = W // MEM_HEADS
    k = k_ref[...].reshape(nseq * mem_tokens, W)
    v = v_ref[...].reshape(nseq * mem_tokens, W)
    q = q_ref[...]
    if nseq > 1:
        rs = lax.broadcasted_iota(jnp.int32, (R, nseq * mem_tokens), 0) // rows_per_seq
        cs = lax.broadcasted_iota(jnp.int32, (R, nseq * mem_tokens), 1) // mem_tokens
        same = rs == cs
    for hh in range(MEM_HEADS):
        sl = slice(hh * hd, (hh + 1) * hd)
        s = lax.dot_general(q[:, sl].astype(BF16), k[:, sl].astype(BF16), (((1,), (1,)), ((), ())),
                            preferred_element_type=F32) * (hd ** -0.5)
        if nseq > 1:
            s = jnp.where(same, s, NEG_INF)
        e = jnp.exp(s - jnp.max(s, axis=-1, keepdims=True))
        p = e / jnp.sum(e, axis=-1, keepdims=True)
        o = jnp.dot(p.astype(BF16), v[:, sl].astype(BF16), preferred_element_type=F32)
        o_ref[:, sl] = o.astype(o_ref.dtype)


def _mem_attn(z, qcol, k, v, kv_spec, cat, *, grid, row_map, nseq, rows_per_seq, tq, mem_tokens, W):
    ncol = cat.shape[1] // W - 1
    qspec = pl.BlockSpec((tq, W), lambda *g: (row_map(*g), qcol))
    ospec = pl.BlockSpec((tq, W), lambda *g: (row_map(*g), ncol))
    return pl.pallas_call(
        functools.partial(_mem_attn_kernel, nseq=nseq, rows_per_seq=rows_per_seq, mem_tokens=mem_tokens),
        grid=grid, in_specs=[qspec, kv_spec[0], kv_spec[1], pl.BlockSpec(memory_space=pl.ANY)], out_specs=ospec,
        out_shape=jax.ShapeDtypeStruct(cat.shape, cat.dtype), input_output_aliases={3: 0},
        compiler_params=_params(("arbitrary",) * len(grid)), name="mem_attn")(z, k, v, cat)


def _swa_kernel(sink_ref, q_ref, kp_ref, vp_ref, kc_ref, vc_ref, cat_ref, o_ref, *, nseq, tq, group, prev_from_grid,
                pos0):
    del cat_ref
    R = nseq * tq
    n_kv = kc_ref.shape[-1] // HD
    q = q_ref[...]
    kp = kp_ref[...].reshape(nseq * WINDOW, n_kv * HD)
    vp = vp_ref[...].reshape(nseq * WINDOW, n_kv * HD)
    kc = kc_ref[...]
    vc = vc_ref[...]
    GR = group * R
    rr = lax.broadcasted_iota(jnp.int32, (GR, nseq * WINDOW), 0) % R
    cp = lax.broadcasted_iota(jnp.int32, (GR, nseq * WINDOW), 1)
    prev_ok = ((rr // tq) == (cp // WINDOW)) & ((cp % WINDOW) > (rr % tq))
    if prev_from_grid:
        prev_ok = prev_ok & (pl.program_id(1) * tq + pos0 - WINDOW + (cp % WINDOW) >= 0)
    else:
        prev_ok = prev_ok & (pos0 - WINDOW + (cp % WINDOW) >= 0)
    rc = lax.broadcasted_iota(jnp.int32, (GR, R), 0) % R
    cc = lax.broadcasted_iota(jnp.int32, (GR, R), 1)
    cur_ok = ((rc // tq) == (cc // tq)) & ((cc % tq) <= (rc % tq))
    gidx = lax.broadcasted_iota(jnp.int32, (GR, 1), 0) // R
    dn = (((1,), (1,)), ((), ()))
    scale = HD ** -0.5
    for hh in range(n_kv):
        ksl = slice(hh * HD, (hh + 1) * HD)
        q3 = jnp.concatenate([q[:, (hh * group + g) * HD:(hh * group + g + 1) * HD] for g in range(group)],
                             axis=0).astype(BF16)
        sp = lax.dot_general(q3, kp[:, ksl].astype(BF16), dn, preferred_element_type=F32) * scale
        sc = lax.dot_general(q3, kc[:, ksl].astype(BF16), dn, preferred_element_type=F32) * scale
        sp = jnp.where(prev_ok, sp, NEG_INF)
        sc = jnp.where(cur_ok, sc, NEG_INF)
        sink = jnp.zeros((GR, 1), F32)
        for g in range(group):
            sink = jnp.where(gidx == g, sink_ref[hh * group + g], sink)
        m = jnp.maximum(jnp.maximum(jnp.max(sp, axis=-1, keepdims=True), jnp.max(sc, axis=-1, keepdims=True)), sink)
        ep = jnp.exp(sp - m)
        ec = jnp.exp(sc - m)
        den = jnp.sum(ep, axis=-1, keepdims=True) + jnp.sum(ec, axis=-1, keepdims=True) + jnp.exp(sink - m)
        o = jnp.dot((ep / den).astype(BF16), vp[:, ksl].astype(BF16), preferred_element_type=F32) + \
            jnp.dot((ec / den).astype(BF16), vc[:, ksl].astype(BF16), preferred_element_type=F32)
        for g in range(group):
            o_ref[:, (hh * group + g) * HD:(hh * group + g + 1) * HD] = o[g * R:(g + 1) * R].astype(o_ref.dtype)


def _swa(sinks, z, kprev, vprev, kvcur, cat, *, grid, row_map, prev_specs, nseq, tq, q_w, kv_w, prev_from_grid, pos0):
    R = nseq * tq
    group = q_w // kv_w
    in_specs = [pl.BlockSpec(memory_space=pltpu.SMEM),
                pl.BlockSpec((R, q_w), lambda *g: (row_map(*g), 0)),
                prev_specs[0], prev_specs[1],
                pl.BlockSpec((R, kv_w), lambda *g: (row_map(*g), 0)),
                pl.BlockSpec((R, kv_w), lambda *g: (row_map(*g), 1)),
                pl.BlockSpec(memory_space=pl.ANY)]
    return pl.pallas_call(
        functools.partial(_swa_kernel, nseq=nseq, tq=tq, group=group, prev_from_grid=prev_from_grid, pos0=pos0),
        grid=grid, in_specs=in_specs, out_specs=pl.BlockSpec((R, q_w), lambda *g: (row_map(*g), 0)),
        out_shape=jax.ShapeDtypeStruct(cat.shape, cat.dtype), input_output_aliases={6: 0},
        compiler_params=_params(("arbitrary",) * len(grid)), name="swa")(sinks, z, kprev, vprev, kvcur, kvcur, cat)


def _ln_silu(y, g, b):
    mu = jnp.mean(y, axis=-1, keepdims=True)
    d = y - mu
    var = jnp.mean(d * d, axis=-1, keepdims=True)
    yn = (d * lax.rsqrt(var + EPS)) * g + b
    return yn * jax.nn.sigmoid(yn)


def _conv_prompt_kernel(a_ref, gt_ref, ah_ref, gh_ref, w_ref, b_ref, lg_ref, lb_ref, o_ref, tail_ref, f_ref, y_ref,
                        *, tt):
    n = pl.program_id(1)
    C = a_ref.shape[1]
    glu = a_ref[...] * jax.nn.sigmoid(gt_ref[...])
    f_ref[HALO:HALO + tt, :] = glu.astype(BF16).astype(F32)
    halo = (ah_ref[...] * jax.nn.sigmoid(gh_ref[...])).astype(BF16).astype(F32)
    f_ref[0:HALO, :] = jnp.where(n > 0, halo, 0.0)

    @pl.when(n == pl.num_programs(1) - 1)
    def _():
        tail_ref[...] = glu[tt - HALO:tt, :]

    def chunk(c, carry):
        lanes = pl.ds(pl.multiple_of(c * LANES, LANES), LANES)
        f = f_ref[:, lanes]
        acc = jnp.zeros((tt // 8, 8, LANES), F32)
        for b in range(8):
            sb = f if b == 0 else pltpu.roll(f, HALO + tt - b, 0)
            for a in range(5):
                w = 8 * a + b - (HALO - CONV_W + 1)
                if 0 <= w < CONV_W:
                    acc = acc + sb[8 * a:8 * a + tt, :].reshape(tt // 8, 8, LANES) * w_ref[w, :, lanes][None]
        y_ref[:, lanes] = acc.reshape(tt, LANES) + b_ref[:, lanes]
        return carry

    lax.fori_loop(0, C // LANES, chunk, 0)
    o_ref[...] = _ln_silu(y_ref[...], lg_ref[...], lb_ref[...]).astype(o_ref.dtype)


def _conv_prompt(z, w_dw, b_dw, ln_g, ln_b, *, batch, seq, C, cat_w, tt):
    M = z.shape[0]
    nt = seq // tt
    hb = tt // HALO
    wrep = jnp.broadcast_to(w_dw[:, None, :], (CONV_W, 8, C))
    vec = pl.BlockSpec((1, C), lambda b, n: (0, 0))

    def halo_row(b, n):
        return jnp.maximum(b * (seq // HALO) + n * hb - 1, b * (seq // HALO))

    return pl.pallas_call(
        functools.partial(_conv_prompt_kernel, tt=tt), grid=(batch, nt),
        in_specs=[pl.BlockSpec((tt, C), lambda b, n: (b * nt + n, 0)),
                  pl.BlockSpec((tt, C), lambda b, n: (b * nt + n, 1)),
                  pl.BlockSpec((HALO, C), lambda b, n: (halo_row(b, n), 0)),
                  pl.BlockSpec((HALO, C), lambda b, n: (halo_row(b, n), 1)),
                  pl.BlockSpec((CONV_W, 8, C), lambda b, n: (0, 0, 0)), vec, vec, vec],
        out_specs=[pl.BlockSpec((tt, C), lambda b, n: (b * nt + n, 0)),
                   pl.BlockSpec((None, HALO, C), lambda b, n: (b, 0, 0))],
        out_shape=[jax.ShapeDtypeStruct((M, cat_w), BF16), jax.ShapeDtypeStruct((batch, HALO, C), F32)],
        scratch_shapes=[pltpu.VMEM((HALO + tt, C), F32), pltpu.VMEM((tt, C), F32)],
        compiler_params=_params(("arbitrary", "arbitrary")), name="conv_prompt")(
            z, z, z, z, wrep, b_dw.reshape(1, C), ln_g.reshape(1, C), ln_b.reshape(1, C))


def _conv_sample_kernel(a_ref, gt_ref, st_ref, wsh_ref, wnew_ref, b_ref, lg_ref, lb_ref, cat_ref, o_ref, glu_ref, *,
                        nseq, tq):
    del cat_ref
    R, C = a_ref.shape
    glu = a_ref[...] * jax.nn.sigmoid(gt_ref[...])
    glu_ref[...] = glu
    glu_b = glu.astype(BF16).astype(F32)
    row = lax.broadcasted_iota(jnp.int32, (R, C), 0)
    y = jnp.zeros((R, C), F32)
    for s in range(nseq):
        buf = st_ref[s].astype(BF16).astype(F32)
        for t in range(tq):
            yt = jnp.sum(buf * wsh_ref[t], axis=0, keepdims=True)
            for u in range(t + 1):
                yt = yt + glu_b[s * tq + u:s * tq + u + 1, :] * wnew_ref[t, u:u + 1, :]
            y = jnp.where(row == s * tq + t, yt, y)
    o_ref[...] = _ln_silu(y + b_ref[...], lg_ref[...], lb_ref[...]).astype(o_ref.dtype)


def _conv_sample(z, state, w_dw, b_dw, ln_g, ln_b, cat, *, n_prompt, dec_seq, C):
    nb, kw1, _ = state.shape
    nseq = 8 // dec_seq
    R = nseq * dec_seq
    base = n_prompt // R
    wsh = jnp.stack([jnp.concatenate([jnp.zeros((t, C), F32), w_dw[:kw1 - t]], axis=0) for t in range(dec_seq)])
    wnew = jnp.stack([jnp.stack([w_dw[kw1 - t + u] if u <= t else jnp.zeros((C,), F32) for u in range(dec_seq)])
                      for t in range(dec_seq)])
    vec = pl.BlockSpec((1, C), lambda s: (0, 0))
    return pl.pallas_call(
        functools.partial(_conv_sample_kernel, nseq=nseq, tq=dec_seq), grid=(nb // nseq,),
        in_specs=[pl.BlockSpec((R, C), lambda s: (base + s, 0)),
                  pl.BlockSpec((R, C), lambda s: (base + s, 1)),
                  pl.BlockSpec((nseq, kw1, C), lambda s: (s, 0, 0)),
                  pl.BlockSpec((dec_seq, kw1, C), lambda s: (0, 0, 0)),
                  pl.BlockSpec((dec_seq, dec_seq, C), lambda s: (0, 0, 0)),
                  vec, vec, vec, pl.BlockSpec(memory_space=pl.ANY)],
        out_specs=[pl.BlockSpec((R, C), lambda s: (base + s, 0)), pl.BlockSpec((R, C), lambda s: (s, 0))],
        out_shape=[jax.ShapeDtypeStruct(cat.shape, cat.dtype), jax.ShapeDtypeStruct((nb * dec_seq, C), F32)],
        input_output_aliases={8: 0},
        compiler_params=_params(("arbitrary",)), name="conv_sample")(
            z, z, state, wsh, wnew, b_dw.reshape(1, C), ln_g.reshape(1, C), ln_b.reshape(1, C), cat)


def _moe_tables(route, n_exp, tm, sub, n_tiles):
    M = route.shape[0]
    e1 = route[:, 0].astype(jnp.int32)
    e2 = route[:, 1].astype(jnp.int32)
    flat_e = jnp.concatenate([e1, e2])
    oh = (flat_e[:, None] == jnp.arange(n_exp, dtype=jnp.int32)[None, :]).astype(jnp.int32)
    csum = jnp.cumsum(oh, axis=0)
    rank = jnp.sum((csum - oh) * oh, axis=1)
    cnt = csum[-1]
    ntile = (cnt + tm - 1) // tm
    tend = jnp.cumsum(ntile)
    tstart = tend - ntile
    pos = jnp.sum(oh * tstart[None, :], axis=1) * tm + rank
    n_valid = tend[-1]
    tid = jnp.arange(n_tiles, dtype=jnp.int32)
    tile_e = jnp.minimum(jnp.sum((tid[:, None] >= tend[None, :]).astype(jnp.int32), axis=1), n_exp - 1)
    rows = jnp.clip(cnt[tile_e] - (tid - tstart[tile_e]) * tm, 0, tm)
    rows = jnp.where(tid < n_valid, rows, 0)
    tile_e = jnp.where(tid < n_valid, tile_e, tile_e[jnp.maximum(n_valid - 1, 0)])
    tile_nsub = (rows + sub - 1) // sub
    nsub_max = tm // sub
    sub_valid = (jnp.arange(nsub_max, dtype=jnp.int32)[None, :] < tile_nsub[:, None]).astype(jnp.int32).reshape(-1)
    tok = jnp.arange(M, dtype=jnp.int32)
    src = jnp.zeros((n_tiles * tm,), jnp.int32).at[pos].set(jnp.concatenate([tok, tok]))
    return (pos[:M], pos[M:], src, sub_valid, tile_e.astype(jnp.int32), rows.astype(jnp.int32),
            n_valid.reshape(1).astype(jnp.int32))


def kernel(x_prompt, x_sample, cache_mem_k, cache_mem_v, state_conv, cache_win_k, cache_win_v, mem_prompt, norm_mix,
           norm_ffn, norm_mem, norm_kv, norm_final, w_mem_kv, w_in_a, w_dw, b_dw, ln_conv_g, ln_conv_b, w_out_a, w_kv,
           w_in_b, sinks, w_out_b, w_gu_dense, w_down_dense, w_router, w_gu_exp, w_down_exp):
    B, T, D = x_prompt.shape
    DB, DT, _ = x_sample.shape
    depth, _, MT, MH, MHD = cache_mem_k.shape
    C = state_conv.shape[-1]
    n_kv = cache_win_k.shape[2]
    KVW = n_kv * HD
    MW = MH * MHD
    n_exp = w_router.shape[-1]
    F = w_down_dense.shape[1]
    NP, NS = B * T, DB * DT
    M = NP + NS
    assert depth == 2 and MH == MEM_HEADS and C % MW == 0 and (8 % DT) == 0 and T % WINDOW == 0
    seqs_per_blk = 8 // DT
    sblk = seqs_per_blk * DT
    sbase = NP // sblk

    bm = _pick_tile(M, 2080, BF16_SUBLANES)
    bn = _pick_tile(MW, 512, LANES)

    mem2d = mem_prompt.reshape(B * MT, D)
    mem_kv = []
    for l in range(depth):
        (mn,) = _addnorm(mem2d, None, [norm_mem[l]], emit_sum=False)
        mem_kv.append(_mm(mn, w_mem_kv[l], bm=B * MT, bn=bn))
    mem_k_p = jnp.stack([kv[:, :MW].reshape(B, MT, MH, MHD) for kv in mem_kv])
    mem_v_p = jnp.stack([kv[:, MW:].reshape(B, MT, MH, MHD) for kv in mem_kv])
    cmk = cache_mem_k.reshape(depth, DB, MT, MW)
    cmv = cache_mem_v.reshape(depth, DB, MT, MW)

    tq_mem = _pick_tile(T, 512, 8)

    def mem_attention(z, qcol, l, cat):
        cat = _mem_attn(z, qcol, mem_kv[l], mem_kv[l],
                        (pl.BlockSpec((MT, MW), lambda b, n: (b, 0)), pl.BlockSpec((MT, MW), lambda b, n: (b, 1))),
                        cat, grid=(B, T // tq_mem), row_map=lambda b, n: b * (T // tq_mem) + n, nseq=1,
                        rows_per_seq=tq_mem, tq=tq_mem, mem_tokens=MT, W=MW)
        sspec = pl.BlockSpec((None, seqs_per_blk, MT, MW), lambda s: (l, s, 0, 0))
        return _mem_attn(z, qcol, cmk, cmv, (sspec, sspec), cat, grid=(DB // seqs_per_blk,),
                         row_map=lambda s: sbase + s, nseq=seqs_per_blk, rows_per_seq=DT, tq=sblk, mem_tokens=MT, W=MW)

    h0 = jnp.concatenate([x_prompt.reshape(NP, D), x_sample.reshape(NS, D)], axis=0)

    (hn,) = _addnorm(h0, None, [norm_mix[0]], emit_sum=False)
    z1 = _mm(hn, w_in_a[0], bm=bm, bn=bn)
    cat, tail = _conv_prompt(z1, w_dw[0], b_dw[0], ln_conv_g[0], ln_conv_b[0], batch=B, seq=T, C=C, cat_w=C + MW,
                             tt=WINDOW)
    cat, glu_s = _conv_sample(z1, state_conv[0], w_dw[0], b_dw[0], ln_conv_g[0], ln_conv_b[0], cat, n_prompt=NP,
                              dec_seq=DT, C=C)
    cat = mem_attention(z1, 2 * C // MW, 0, cat)
    y = _mm(cat, w_out_a[0], bm=bm, bn=bn)
    h1, hn = _addnorm(h0, y, [norm_ffn[0]], emit_sum=True)
    dense_tabs = (jnp.zeros((M // bm,), jnp.int32), jnp.full((M // bm,), bm, jnp.int32),
                  jnp.full((1,), M // bm, jnp.int32))
    act = _glu_up(hn, w_gu_dense, *dense_tabs, bm=bm, chunk=bm, gran=bm, bn=_pick_tile(F, 256, LANES))
    y = _down(act, w_down_dense, *dense_tabs, bm=bm, chunk=bm, gran=bm, bn=_pick_tile(D, 1024, LANES),
              bk=_pick_tile(F, 1024, LANES))
    h2, hn_kv, hn = _addnorm(h1, y, [norm_kv, norm_mix[1]], emit_sum=True)

    kv = _mm(hn_kv, w_kv, bm=bm, bn=bn)
    k_new, v_new = kv[:, :KVW], kv[:, KVW:]

    z2 = _mm(hn, w_in_b[0], bm=bm, bn=bn)
    nblk = T // WINDOW
    cat = jnp.zeros((M, C + MW), BF16)

    def prev_row(b, n):
        return jnp.maximum(b * nblk + n - 1, b * nblk)

    cat = _swa(sinks[0], z2, kv, kv, kv, cat, grid=(B, nblk), row_map=lambda b, n: b * nblk + n,
               prev_specs=(pl.BlockSpec((WINDOW, KVW), lambda b, n: (prev_row(b, n), 0)),
                           pl.BlockSpec((WINDOW, KVW), lambda b, n: (prev_row(b, n), 1))),
               nseq=1, tq=WINDOW, q_w=C, kv_w=KVW, prev_from_grid=True, pos0=0)
    pspec = pl.BlockSpec((seqs_per_blk, WINDOW, KVW), lambda s: (s, 0, 0))
    cat = _swa(sinks[0], z2, cache_win_k.reshape(DB, WINDOW, KVW), cache_win_v.reshape(DB, WINDOW, KVW), kv, cat,
               grid=(DB // seqs_per_blk,), row_map=lambda s: sbase + s, prev_specs=(pspec, pspec),
               nseq=seqs_per_blk, tq=DT, q_w=C, kv_w=KVW, prev_from_grid=False, pos0=PAST_LEN)
    cat = mem_attention(z2, C // MW, 1, cat)
    y = _mm(cat, w_out_b[0], bm=bm, bn=bn)
    h3, route = _router(h2, y, norm_ffn[1], w_router[0])

    mean_rows = TOP_K * M // n_exp
    gran = 128 if mean_rows >= 1024 else 16
    chunk = 4 * gran
    tm = -(-(mean_rows * 29 // 20) // chunk) * chunk
    n_tiles = TOP_K * M // tm + n_exp
    pos1, pos2, src, sub_valid, tile_e, tile_rows, n_valid = _moe_tables(route, n_exp, tm, gran, n_tiles)
    xs = _gather_norm(h3, norm_ffn[1], src, sub_valid, sub=gran)
    act = _glu_up(xs, w_gu_exp[0], tile_e, tile_rows, n_valid, bm=tm, chunk=chunk, gran=gran,
                  bn=_pick_tile(F, 256, LANES))
    ys = _down(act, w_down_exp[0], tile_e, tile_rows, n_valid, bm=tm, chunk=chunk, gran=gran,
               bn=_pick_tile(D, 1024, LANES), bk=_pick_tile(F, 1024, LANES))
    y_p, y_s = _combine(h3, ys, pos1, pos2, route[:, 2:3], route[:, 3:4], norm_final, n_prompt=NP,
                        tm=_pick_tile(NS, 128, 8))

    conv_p = tail[None, :, HALO - (CONV_W - 1):, :]
    conv_s = jnp.concatenate([state_conv[0][:, DT:], glu_s.reshape(DB, DT, C)], axis=1)[None]
    kp = k_new[:NP].reshape(B, T, n_kv, HD)[:, T - WINDOW:]
    vp = v_new[:NP].reshape(B, T, n_kv, HD)[:, T - WINDOW:]
    ks = jnp.concatenate([cache_win_k, k_new[NP:].reshape(DB, DT, n_kv, HD)], axis=1)[:, -WINDOW:]
    vs = jnp.concatenate([cache_win_v, v_new[NP:].reshape(DB, DT, n_kv, HD)], axis=1)[:, -WINDOW:]
    return (y_p.reshape(B, T, D), y_s.reshape(DB, DT, D), mem_k_p, mem_v_p, conv_p, kp, vp, conv_s, ks, vs)
```

```python
import functools

import jax
import jax.numpy as jnp
from jax import lax
from jax.experimental import pallas as pl
from jax.experimental.pallas import tpu as pltpu

BF16 = jnp.bfloat16
F32 = jnp.float32
EPS = 1e-5
NEG_INF = -1e30
WINDOW = 128
HD = 128
MEM_HEADS = 4
CONV_W = 31
TOP_K = 2
PAST_LEN = 16384

V7X_VMEM_LIMIT_BYTES = 60 * 1024 * 1024
LANES = 128
BF16_SUBLANES = 16
HALO = 32


def _params(sem):
    return pltpu.CompilerParams(dimension_semantics=sem, vmem_limit_bytes=V7X_VMEM_LIMIT_BYTES)


def _pick_tile(n, target, mult):
    best = None
    for t in range(mult, min(n, target) + 1, mult):
        if n % t == 0:
            best = t
    assert best is not None, (n, target, mult)
    return best


def _addnorm_kernel(*refs, has_res, emit_sum, n_gain):
    it = iter(refs)
    h = next(it)[...]
    if has_res:
        h = h + next(it)[...]
    gains = [next(it) for _ in range(n_gain)]
    if emit_sum:
        next(it)[...] = h
    inv = lax.rsqrt(jnp.mean(h * h, axis=-1, keepdims=True) + EPS)
    for g in gains:
        o = next(it)
        o[...] = ((h * inv) * g[...]).astype(o.dtype)


def _addnorm(h, res, gains, *, emit_sum, out_dtype=BF16):
    M, D = h.shape
    tm = _pick_tile(M, 208, BF16_SUBLANES)
    row = pl.BlockSpec((tm, D), lambda i: (i, 0))
    gspec = pl.BlockSpec((1, D), lambda i: (0, 0))
    ins = [h] + ([res] if res is not None else []) + [g.reshape(1, D) for g in gains]
    in_specs = [row] + ([row] if res is not None else []) + [gspec] * len(gains)
    out_shape = ([jax.ShapeDtypeStruct((M, D), F32)] if emit_sum else []) + \
        [jax.ShapeDtypeStruct((M, D), out_dtype) for _ in gains]
    outs = pl.pallas_call(
        functools.partial(_addnorm_kernel, has_res=res is not None, emit_sum=emit_sum, n_gain=len(gains)),
        grid=(M // tm,), in_specs=in_specs, out_specs=[row] * len(out_shape), out_shape=out_shape,
        compiler_params=_params(("arbitrary",)), name="addnorm")(*ins)
    return outs


def _router_kernel(h_ref, y_ref, g_ref, wr_ref, hs_ref, o_ref, *, n_exp):
    h = h_ref[...] + y_ref[...]
    hs_ref[...] = h
    hn = (h * lax.rsqrt(jnp.mean(h * h, axis=-1, keepdims=True) + EPS)) * g_ref[...]
    logits = jnp.dot(hn.astype(BF16), wr_ref[...].astype(BF16), preferred_element_type=F32)
    lane = lax.broadcasted_iota(jnp.int32, logits.shape, 1)
    l1 = jnp.where(lane < n_exp, logits, -jnp.inf)
    m1 = jnp.max(l1, axis=-1, keepdims=True)
    i1 = jnp.min(jnp.where(l1 == m1, lane, LANES), axis=-1, keepdims=True)
    l2 = jnp.where(lane == i1, -jnp.inf, l1)
    m2 = jnp.max(l2, axis=-1, keepdims=True)
    i2 = jnp.min(jnp.where(l2 == m2, lane, LANES), axis=-1, keepdims=True)
    e = jnp.exp(m2 - m1)
    g1 = 1.0 / (1.0 + e)
    g2 = e / (1.0 + e)
    o_ref[...] = jnp.where(lane == 0, i1.astype(F32),
                           jnp.where(lane == 1, i2.astype(F32),
                                     jnp.where(lane == 2, g1, jnp.where(lane == 3, g2, 0.0))))


def _router(h, res, gain, w_router):
    M, D = h.shape
    n_exp = w_router.shape[1]
    wr = jnp.pad(w_router, ((0, 0), (0, LANES - n_exp)))
    tm = _pick_tile(M, 208, BF16_SUBLANES)
    row = pl.BlockSpec((tm, D), lambda i: (i, 0))
    return pl.pallas_call(
        functools.partial(_router_kernel, n_exp=n_exp),
        grid=(M // tm,),
        in_specs=[row, row, pl.BlockSpec((1, D), lambda i: (0, 0)), pl.BlockSpec((D, LANES), lambda i: (0, 0))],
        out_specs=[row, pl.BlockSpec((tm, LANES), lambda i: (i, 0))],
        out_shape=[jax.ShapeDtypeStruct((M, D), F32), jax.ShapeDtypeStruct((M, LANES), F32)],
        compiler_params=_params(("arbitrary",)), name="router")(h, res, gain.reshape(1, D), wr)


W_STREAMS = 4


def _cast_bands(w_refs, wb_ref):
    kb = wb_ref.shape[0] // len(w_refs)
    for q, w_ref in enumerate(w_refs):
        wb_ref[q * kb:(q + 1) * kb, :] = w_ref[...].reshape(kb, wb_ref.shape[1]).astype(BF16)


def _mm_kernel(x_ref, *rest):
    *w_refs, o_ref, wb_ref = rest
    _cast_bands(w_refs, wb_ref)
    o_ref[...] = jnp.dot(x_ref[...], wb_ref[...], preferred_element_type=F32).astype(o_ref.dtype)


def _mm(x, w, *, bm, bn, out_dtype=F32):
    M, K = x.shape
    N = w.shape[1]
    kb = K // W_STREAMS
    return pl.pallas_call(
        _mm_kernel, grid=(M // bm, N // bn),
        in_specs=[pl.BlockSpec((bm, K), lambda i, j: (i, 0), pipeline_mode=pl.Buffered(1))] +
                 [pl.BlockSpec((kb, bn), functools.partial(lambda i, j, q: (q, j), q=q)) for q in range(W_STREAMS)],
        out_specs=pl.BlockSpec((bm, bn), lambda i, j: (i, j)),
        out_shape=jax.ShapeDtypeStruct((M, N), out_dtype),
        scratch_shapes=[pltpu.VMEM((K, bn), BF16)],
        compiler_params=_params(("arbitrary", "arbitrary")), name="mm")(x, *([w] * W_STREAMS))


def _row_loops(need, chunk, gran, fn):
    total = (need + gran - 1) // gran * gran
    nc = total // chunk

    def big(c, carry):
        fn(pl.multiple_of(c * chunk, chunk), chunk)
        return carry

    lax.fori_loop(0, nc, big, 0)
    base = nc * chunk
    size = chunk // 2
    while size >= gran:
        take = (total - base) >= size
        pl.when(take)(functools.partial(fn, pl.multiple_of(base, size), size))
        base = base + jnp.where(take, size, 0)
        size //= 2
    return total


def _glu_up_kernel(te_ref, ns_ref, nv_ref, x_ref, *rest, chunk, gran):
    *w_refs, o_ref, wgb_ref, wub_ref = rest
    half = len(w_refs) // 2
    need = ns_ref[pl.program_id(0)]
    bm = o_ref.shape[0]

    def rows(r0, n):
        xs = x_ref[pl.ds(r0, n), :]
        g = jnp.dot(xs, wgb_ref[...], preferred_element_type=F32)
        u = jnp.dot(xs, wub_ref[...], preferred_element_type=F32)
        o_ref[pl.ds(r0, n), :] = ((g * jax.nn.sigmoid(g)) * u).astype(o_ref.dtype)

    @pl.when(need > 0)
    def _():
        _cast_bands(w_refs[:half], wgb_ref)
        _cast_bands(w_refs[half:], wub_ref)
        done = _row_loops(need, chunk, gran, rows)

        def zero(g, carry):
            o_ref[pl.ds(pl.multiple_of(done + g * gran, gran), gran), :] = jnp.zeros((gran, o_ref.shape[1]),
                                                                                     o_ref.dtype)
            return carry

        if gran < bm:
            lax.fori_loop(0, (bm - done) // gran, zero, 0)


def _glu_up(x, w_gu, tile_e, tile_rows, n_valid, *, bm, chunk, gran, bn):
    R, K = x.shape
    F = w_gu.shape[2] // 2
    T, J = R // bm, F // bn
    nq = W_STREAMS // 2

    def tsel(t, nv):
        return jnp.minimum(t, nv[0] - 1)

    def jsel(t, j, nv):
        return jnp.where(t < nv[0], j, J - 1)

    gs = pltpu.PrefetchScalarGridSpec(
        num_scalar_prefetch=3, grid=(T, J),
        in_specs=[pl.BlockSpec((bm, K), lambda t, j, te, ns, nv: (tsel(t, nv), 0), pipeline_mode=pl.Buffered(1))] +
                 [pl.BlockSpec((1, K // nq, bn), functools.partial(
                     lambda t, j, te, ns, nv, q, off: (te[t], q, off + jsel(t, j, nv)), q=q, off=off))
                  for off in (0, J) for q in range(nq)],
        out_specs=pl.BlockSpec((bm, bn), lambda t, j, te, ns, nv: (tsel(t, nv), jsel(t, j, nv))),
        scratch_shapes=[pltpu.VMEM((K, bn), BF16), pltpu.VMEM((K, bn), BF16)])
    return pl.pallas_call(
        functools.partial(_glu_up_kernel, chunk=chunk, gran=gran), grid_spec=gs,
        out_shape=jax.ShapeDtypeStruct((R, F), BF16),
        compiler_params=_params(("arbitrary", "arbitrary")), name="glu_up")(
            tile_e, tile_rows, n_valid, x, *([w_gu] * (2 * nq)))


def _down_kernel(te_ref, ns_ref, nv_ref, x_ref, w_ref, o_ref, wb_ref, *, chunk, gran):
    need = ns_ref[pl.program_id(0)]
    k = pl.program_id(2)

    def rows(r0, n):
        o_ref[pl.ds(r0, n), :] += jnp.dot(x_ref[pl.ds(r0, n), :], wb_ref[...], preferred_element_type=F32)

    @pl.when(need > 0)
    def _():
        wb_ref[...] = w_ref[0].astype(BF16)

        @pl.when(k == 0)
        def _():
            o_ref[...] = jnp.zeros(o_ref.shape, o_ref.dtype)

        _row_loops(need, chunk, gran, rows)


def _down(x, w, tile_e, tile_rows, n_valid, *, bm, chunk, gran, bn, bk):
    R, F = x.shape
    N = w.shape[2]
    T, J, KK = R // bm, N // bn, F // bk

    def tsel(t, nv):
        return jnp.minimum(t, nv[0] - 1)

    def sel(t, a, last, nv):
        return jnp.where(t < nv[0], a, last)

    gs = pltpu.PrefetchScalarGridSpec(
        num_scalar_prefetch=3, grid=(T, J, KK),
        in_specs=[pl.BlockSpec((bm, bk), lambda t, j, k, te, ns, nv: (tsel(t, nv), sel(t, k, KK - 1, nv))),
                  pl.BlockSpec((1, bk, bn),
                               lambda t, j, k, te, ns, nv: (te[t], sel(t, k, KK - 1, nv), sel(t, j, J - 1, nv)))],
        out_specs=pl.BlockSpec((bm, bn), lambda t, j, k, te, ns, nv: (tsel(t, nv), sel(t, j, J - 1, nv))),
        scratch_shapes=[pltpu.VMEM((bk, bn), BF16)])
    return pl.pallas_call(
        functools.partial(_down_kernel, chunk=chunk, gran=gran), grid_spec=gs,
        out_shape=jax.ShapeDtypeStruct((R, N), F32),
        compiler_params=_params(("arbitrary", "arbitrary", "arbitrary")), name="down")(
            tile_e, tile_rows, n_valid, x, w)


def _gather_norm_kernel(valid_ref, idx_ref, idx_next_ref, h_hbm, g_ref, o_ref, buf, sem, *, sub):
    s = pl.program_id(0)
    n_steps = pl.num_programs(0)
    slot = s % 2

    def row_copy(src_row, sl, r):
        return pltpu.make_async_copy(h_hbm.at[src_row], buf.at[sl, r], sem.at[sl])

    def start_rows(idx, sl):
        def body(r, c):
            row_copy(idx[0, 0, r], sl, r).start()
            return c

        lax.fori_loop(0, sub, body, 0, unroll=8)

    @pl.when((s == 0) & (valid_ref[0] > 0))
    def _():
        start_rows(idx_ref, 0)

    @pl.when((s + 1 < n_steps) & (valid_ref[jnp.minimum(s + 1, n_steps - 1)] > 0))
    def _():
        start_rows(idx_next_ref, 1 - slot)

    @pl.when(valid_ref[s] > 0)
    def _():
        def wait(r, c):
            row_copy(0, slot, 0).wait()
            return c

        lax.fori_loop(0, sub, wait, 0, unroll=8)
        h = buf[slot]
        inv = lax.rsqrt(jnp.mean(h * h, axis=-1, keepdims=True) + EPS)
        o_ref[...] = ((h * inv) * g_ref[...]).astype(o_ref.dtype)

    @pl.when(valid_ref[s] == 0)
    def _():
        o_ref[...] = jnp.zeros(o_ref.shape, o_ref.dtype)


def _gather_norm(h, gain, src_idx, sub_valid, *, sub):
    M, D = h.shape
    S = sub_valid.shape[0]
    gs = pltpu.PrefetchScalarGridSpec(
        num_scalar_prefetch=1, grid=(S,),
        in_specs=[pl.BlockSpec((1, 1, sub), lambda s, v: (s, 0, 0), memory_space=pltpu.SMEM),
                  pl.BlockSpec((1, 1, sub), lambda s, v: (jnp.minimum(s + 1, S - 1), 0, 0), memory_space=pltpu.SMEM),
                  pl.BlockSpec(memory_space=pl.ANY),
                  pl.BlockSpec((1, D), lambda s, v: (0, 0))],
        out_specs=pl.BlockSpec((sub, D), lambda s, v: (s, 0)),
        scratch_shapes=[pltpu.VMEM((2, sub, D), F32), pltpu.SemaphoreType.DMA((2,))])
    idx3 = src_idx.reshape(S, 1, sub)
    return pl.pallas_call(
        functools.partial(_gather_norm_kernel, sub=sub), grid_spec=gs,
        out_shape=jax.ShapeDtypeStruct((S * sub, D), BF16),
        compiler_params=_params(("arbitrary",)), name="gather_norm")(
            sub_valid, idx3, idx3, h, gain.reshape(1, D))


def _combine_kernel(p1_ref, p2_ref, p1n_ref, p2n_ref, h_ref, g1_ref, g2_ref, y_hbm, gain_ref, op_ref, os_ref, buf, sem,
                    *, tm, n_prompt_tiles):
    i = pl.program_id(0)
    slot = i % 2

    def row_copy(src_row, sl, which, r):
        return pltpu.make_async_copy(y_hbm.at[src_row], buf.at[sl, which, r], sem.at[sl])

    def start_rows(pa, pb, sl):
        def body(r, c):
            row_copy(pa[0, 0, r], sl, 0, r).start()
            row_copy(pb[0, 0, r], sl, 1, r).start()
            return c

        lax.fori_loop(0, tm, body, 0, unroll=8)

    @pl.when(i == 0)
    def _():
        start_rows(p1_ref, p2_ref, 0)

    @pl.when(i + 1 < pl.num_programs(0))
    def _():
        start_rows(p1n_ref, p2n_ref, 1 - slot)

    def wait(r, c):
        row_copy(0, slot, 0, 0).wait()
        row_copy(0, slot, 0, 0).wait()
        return c

    lax.fori_loop(0, tm, wait, 0, unroll=8)
    h = h_ref[...] + (buf[slot, 0] * g1_ref[...] + buf[slot, 1] * g2_ref[...])
    y = ((h * lax.rsqrt(jnp.mean(h * h, axis=-1, keepdims=True) + EPS)) * gain_ref[...])

    @pl.when(i < n_prompt_tiles)
    def _():
        op_ref[...] = y

    @pl.when(i >= n_prompt_tiles)
    def _():
        os_ref[...] = y


def _combine(h, ys, pos1, pos2, g1, g2, gain, *, n_prompt, tm):
    M, D = h.shape
    n_tiles = M // tm
    npt = n_prompt // tm
    row = pl.BlockSpec((tm, D), lambda i: (i, 0))
    col = pl.BlockSpec((tm, 1), lambda i: (i, 0))
    idx = pl.BlockSpec((1, 1, tm), lambda i: (i, 0, 0), memory_space=pltpu.SMEM)
    idx_next = pl.BlockSpec((1, 1, tm), lambda i: (jnp.minimum(i + 1, n_tiles - 1), 0, 0), memory_space=pltpu.SMEM)
    p1, p2 = pos1.reshape(n_tiles, 1, tm), pos2.reshape(n_tiles, 1, tm)
    return pl.pallas_call(
        functools.partial(_combine_kernel, tm=tm, n_prompt_tiles=npt), grid=(n_tiles,),
        in_specs=[idx, idx, idx_next, idx_next, row, col, col, pl.BlockSpec(memory_space=pl.ANY),
                  pl.BlockSpec((1, D), lambda i: (0, 0))],
        out_specs=[pl.BlockSpec((tm, D), lambda i: (jnp.minimum(i, npt - 1), 0)),
                   pl.BlockSpec((tm, D), lambda i: (jnp.maximum(i - npt, 0), 0))],
        out_shape=[jax.ShapeDtypeStruct((n_prompt, D), F32), jax.ShapeDtypeStruct((M - n_prompt, D), F32)],
        scratch_shapes=[pltpu.VMEM((2, 2, tm, D), F32), pltpu.SemaphoreType.DMA((2,))],
        compiler_params=_params(("arbitrary",)), name="combine")(
            p1, p2, p1, p2, h, g1, g2, ys, gain.reshape(1, D))


def _mem_attn_kernel(q_ref, k_ref, v_ref, cat_ref, o_ref, *, nseq, rows_per_seq, mem_tokens):
    del cat_ref
    R, W = q_ref.shape
    hd = W // MEM_HEADS
    k = k_ref[...].reshape(nseq * mem_tokens, W)
    v = v_ref[...].reshape(nseq * mem_tokens, W)
    q = q_ref[...]
    if nseq > 1:
        rs = lax.broadcasted_iota(jnp.int32, (R, nseq * mem_tokens), 0) // rows_per_seq
        cs = lax.broadcasted_iota(jnp.int32, (R, nseq * mem_tokens), 1) // mem_tokens
        same = rs == cs
    for hh in range(MEM_HEADS):
        sl = slice(hh * hd, (hh + 1) * hd)
        s = lax.dot_general(q[:, sl].astype(BF16), k[:, sl].astype(BF16), (((1,), (1,)), ((), ())),
                            preferred_element_type=F32) * (hd ** -0.5)
        if nseq > 1:
            s = jnp.where(same, s, NEG_INF)
        e = jnp.exp(s - jnp.max(s, axis=-1, keepdims=True))
        p = e / jnp.sum(e, axis=-1, keepdims=True)
        o = jnp.dot(p.astype(BF16), v[:, sl].astype(BF16), preferred_element_type=F32)
        o_ref[:, sl] = o.astype(o_ref.dtype)


def _mem_attn(z, qcol, k, v, kv_spec, cat, *, grid, row_map, nseq, rows_per_seq, tq, mem_tokens, W):
    ncol = cat.shape[1] // W - 1
    qspec = pl.BlockSpec((tq, W), lambda *g: (row_map(*g), qcol))
    ospec = pl.BlockSpec((tq, W), lambda *g: (row_map(*g), ncol))
    return pl.pallas_call(
        functools.partial(_mem_attn_kernel, nseq=nseq, rows_per_seq=rows_per_seq, mem_tokens=mem_tokens),
        grid=grid, in_specs=[qspec, kv_spec[0], kv_spec[1], pl.BlockSpec(memory_space=pl.ANY)], out_specs=ospec,
        out_shape=jax.ShapeDtypeStruct(cat.shape, cat.dtype), input_output_aliases={3: 0},
        compiler_params=_params(("arbitrary",) * len(grid)), name="mem_attn")(z, k, v, cat)


def _swa_kernel(sink_ref, q_ref, kp_ref, vp_ref, kc_ref, vc_ref, cat_ref, o_ref, *, nseq, tq, group, prev_from_grid,
                pos0):
    del cat_ref
    R = nseq * tq
    n_kv = kc_ref.shape[-1] // HD
    q = q_ref[...]
    kp = kp_ref[...].reshape(nseq * WINDOW, n_kv * HD)
    vp = vp_ref[...].reshape(nseq * WINDOW, n_kv * HD)
    kc = kc_ref[...]
    vc = vc_ref[...]
    GR = group * R
    rr = lax.broadcasted_iota(jnp.int32, (GR, nseq * WINDOW), 0) % R
    cp = lax.broadcasted_iota(jnp.int32, (GR, nseq * WINDOW), 1)
    prev_ok = ((rr // tq) == (cp // WINDOW)) & ((cp % WINDOW) > (rr % tq))
    if prev_from_grid:
        prev_ok = prev_ok & (pl.program_id(1) * tq + pos0 - WINDOW + (cp % WINDOW) >= 0)
    else:
        prev_ok = prev_ok & (pos0 - WINDOW + (cp % WINDOW) >= 0)
    rc = lax.broadcasted_iota(jnp.int32, (GR, R), 0) % R
    cc = lax.broadcasted_iota(jnp.int32, (GR, R), 1)
    cur_ok = ((rc // tq) == (cc // tq)) & ((cc % tq) <= (rc % tq))
    gidx = lax.broadcasted_iota(jnp.int32, (GR, 1), 0) // R
    dn = (((1,), (1,)), ((), ()))
    scale = HD ** -0.5
    for hh in range(n_kv):
        ksl = slice(hh * HD, (hh + 1) * HD)
        q3 = jnp.concatenate([q[:, (hh * group + g) * HD:(hh * group + g + 1) * HD] for g in range(group)],
                             axis=0).astype(BF16)
        sp = lax.dot_general(q3, kp[:, ksl].astype(BF16), dn, preferred_element_type=F32) * scale
        sc = lax.dot_general(q3, kc[:, ksl].astype(BF16), dn, preferred_element_type=F32) * scale
        sp = jnp.where(prev_ok, sp, NEG_INF)
        sc = jnp.where(cur_ok, sc, NEG_INF)
        sink = jnp.zeros((GR, 1), F32)
        for g in range(group):
            sink = jnp.where(gidx == g, sink_ref[hh * group + g], sink)
        m = jnp.maximum(jnp.maximum(jnp.max(sp, axis=-1, keepdims=True), jnp.max(sc, axis=-1, keepdims=True)), sink)
        ep = jnp.exp(sp - m)
        ec = jnp.exp(sc - m)
        den = jnp.sum(ep, axis=-1, keepdims=True) + jnp.sum(ec, axis=-1, keepdims=True) + jnp.exp(sink - m)
        o = jnp.dot((ep / den).astype(BF16), vp[:, ksl].astype(BF16), preferred_element_type=F32) + \
            jnp.dot((ec / den).astype(BF16), vc[:, ksl].astype(BF16), preferred_element_type=F32)
        for g in range(group):
            o_ref[:, (hh * group + g) * HD:(hh * group + g + 1) * HD] = o[g * R:(g + 1) * R].astype(o_ref.dtype)


def _swa(sinks, z, kprev, vprev, kvcur, cat, *, grid, row_map, prev_specs, nseq, tq, q_w, kv_w, prev_from_grid, pos0):
    R = nseq * tq
    group = q_w // kv_w
    in_specs = [pl.BlockSpec(memory_space=pltpu.SMEM),
                pl.BlockSpec((R, q_w), lambda *g: (row_map(*g), 0)),
                prev_specs[0], prev_specs[1],
                pl.BlockSpec((R, kv_w), lambda *g: (row_map(*g), 0)),
                pl.BlockSpec((R, kv_w), lambda *g: (row_map(*g), 1)),
                pl.BlockSpec(memory_space=pl.ANY)]
    return pl.pallas_call(
        functools.partial(_swa_kernel, nseq=nseq, tq=tq, group=group, prev_from_grid=prev_from_grid, pos0=pos0),
        grid=grid, in_specs=in_specs, out_specs=pl.BlockSpec((R, q_w), lambda *g: (row_map(*g), 0)),
        out_shape=jax.ShapeDtypeStruct(cat.shape, cat.dtype), input_output_aliases={6: 0},
        compiler_params=_params(("arbitrary",) * len(grid)), name="swa")(sinks, z, kprev, vprev, kvcur, kvcur, cat)


def _ln_silu(y, g, b):
    mu = jnp.mean(y, axis=-1, keepdims=True)
    d = y - mu
    var = jnp.mean(d * d, axis=-1, keepdims=True)
    yn = (d * lax.rsqrt(var + EPS)) * g + b
    return yn * jax.nn.sigmoid(yn)


def _conv_prompt_kernel(a_ref, gt_ref, ah_ref, gh_ref, w_ref, b_ref, lg_ref, lb_ref, o_ref, tail_ref, f_ref, y_ref,
                        *, tt):
    n = pl.program_id(1)
    C = a_ref.shape[1]
    glu = a_ref[...] * jax.nn.sigmoid(gt_ref[...])
    f_ref[HALO:HALO + tt, :] = glu.astype(BF16).astype(F32)
    halo = (ah_ref[...] * jax.nn.sigmoid(gh_ref[...])).astype(BF16).astype(F32)
    f_ref[0:HALO, :] = jnp.where(n > 0, halo, 0.0)

    @pl.when(n == pl.num_programs(1) - 1)
    def _():
        tail_ref[...] = glu[tt - HALO:tt, :]

    def chunk(c, carry):
        lanes = pl.ds(pl.multiple_of(c * LANES, LANES), LANES)
        f = f_ref[:, lanes]
        acc = jnp.zeros((tt // 8, 8, LANES), F32)
        for b in range(8):
            sb = f if b == 0 else pltpu.roll(f, HALO + tt - b, 0)
            for a in range(5):
                w = 8 * a + b - (HALO - CONV_W + 1)
                if 0 <= w < CONV_W:
                    acc = acc + sb[8 * a:8 * a + tt, :].reshape(tt // 8, 8, LANES) * w_ref[w, :, lanes][None]
        y_ref[:, lanes] = acc.reshape(tt, LANES) + b_ref[:, lanes]
        return carry

    lax.fori_loop(0, C // LANES, chunk, 0)
    o_ref[...] = _ln_silu(y_ref[...], lg_ref[...], lb_ref[...]).astype(o_ref.dtype)


def _conv_prompt(z, w_dw, b_dw, ln_g, ln_b, *, batch, seq, C, cat_w, tt):
    M = z.shape[0]
    nt = seq // tt
    hb = tt // HALO
    wrep = jnp.broadcast_to(w_dw[:, None, :], (CONV_W, 8, C))
    vec = pl.BlockSpec((1, C), lambda b, n: (0, 0))

    def halo_row(b, n):
        return jnp.maximum(b * (seq // HALO) + n * hb - 1, b * (seq // HALO))

    return pl.pallas_call(
        functools.partial(_conv_prompt_kernel, tt=tt), grid=(batch, nt),
        in_specs=[pl.BlockSpec((tt, C), lambda b, n: (b * nt + n, 0)),
                  pl.BlockSpec((tt, C), lambda b, n: (b * nt + n, 1)),
                  pl.BlockSpec((HALO, C), lambda b, n: (halo_row(b, n), 0)),
                  pl.BlockSpec((HALO, C), lambda b, n: (halo_row(b, n), 1)),
                  pl.BlockSpec((CONV_W, 8, C), lambda b, n: (0, 0, 0)), vec, vec, vec],
        out_specs=[pl.BlockSpec((tt, C), lambda b, n: (b * nt + n, 0)),
                   pl.BlockSpec((None, HALO, C), lambda b, n: (b, 0, 0))],
        out_shape=[jax.ShapeDtypeStruct((M, cat_w), BF16), jax.ShapeDtypeStruct((batch, HALO, C), F32)],
        scratch_shapes=[pltpu.VMEM((HALO + tt, C), F32), pltpu.VMEM((tt, C), F32)],
        compiler_params=_params(("arbitrary", "arbitrary")), name="conv_prompt")(
            z, z, z, z, wrep, b_dw.reshape(1, C), ln_g.reshape(1, C), ln_b.reshape(1, C))


def _conv_sample_kernel(a_ref, gt_ref, st_ref, wsh_ref, wnew_ref, b_ref, lg_ref, lb_ref, cat_ref, o_ref, glu_ref, *,
                        nseq, tq):
    del cat_ref
    R, C = a_ref.shape
    glu = a_ref[...] * jax.nn.sigmoid(gt_ref[...])
    glu_ref[...] = glu
    glu_b = glu.astype(BF16).astype(F32)
    row = lax.broadcasted_iota(jnp.int32, (R, C), 0)
    y = jnp.zeros((R, C), F32)
    for s in range(nseq):
        buf = st_ref[s].astype(BF16).astype(F32)
        for t in range(tq):
            yt = jnp.sum(buf * wsh_ref[t], axis=0, keepdims=True)
            for u in range(t + 1):
                yt = yt + glu_b[s * tq + u:s * tq + u + 1, :] * wnew_ref[t, u:u + 1, :]
            y = jnp.where(row == s * tq + t, yt, y)
    o_ref[...] = _ln_silu(y + b_ref[...], lg_ref[...], lb_ref[...]).astype(o_ref.dtype)


def _conv_sample(z, state, w_dw, b_dw, ln_g, ln_b, cat, *, n_prompt, dec_seq, C):
    nb, kw1, _ = state.shape
    nseq = 8 // dec_seq
    R = nseq * dec_seq
    base = n_prompt // R
    wsh = jnp.stack([jnp.concatenate([jnp.zeros((t, C), F32), w_dw[:kw1 - t]], axis=0) for t in range(dec_seq)])
    wnew = jnp.stack([jnp.stack([w_dw[kw1 - t + u] if u <= t else jnp.zeros((C,), F32) for u in range(dec_seq)])
                      for t in range(dec_seq)])
    vec = pl.BlockSpec((1, C), lambda s: (0, 0))
    return pl.pallas_call(
        functools.partial(_conv_sample_kernel, nseq=nseq, tq=dec_seq), grid=(nb // nseq,),
        in_specs=[pl.BlockSpec((R, C), lambda s: (base + s, 0)),
                  pl.BlockSpec((R, C), lambda s: (base + s, 1)),
                  pl.BlockSpec((nseq, kw1, C), lambda s: (s, 0, 0)),
                  pl.BlockSpec((dec_seq, kw1, C), lambda s: (0, 0, 0)),
                  pl.BlockSpec((dec_seq, dec_seq, C), lambda s: (0, 0, 0)),
                  vec, vec, vec, pl.BlockSpec(memory_space=pl.ANY)],
        out_specs=[pl.BlockSpec((R, C), lambda s: (base + s, 0)), pl.BlockSpec((R, C), lambda s: (s, 0))],
        out_shape=[jax.ShapeDtypeStruct(cat.shape, cat.dtype), jax.ShapeDtypeStruct((nb * dec_seq, C), F32)],
        input_output_aliases={8: 0},
        compiler_params=_params(("arbitrary",)), name="conv_sample")(
            z, z, state, wsh, wnew, b_dw.reshape(1, C), ln_g.reshape(1, C), ln_b.reshape(1, C), cat)


def _moe_tables(route, n_exp, tm, sub, n_tiles):
    M = route.shape[0]
    e1 = route[:, 0].astype(jnp.int32)
    e2 = route[:, 1].astype(jnp.int32)
    flat_e = jnp.concatenate([e1, e2])
    oh = (flat_e[:, None] == jnp.arange(n_exp, dtype=jnp.int32)[None, :]).astype(jnp.int32)
    csum = jnp.cumsum(oh, axis=0)
    rank = jnp.sum((csum - oh) * oh, axis=1)
    cnt = csum[-1]
    ntile = (cnt + tm - 1) // tm
    tend = jnp.cumsum(ntile)
    tstart = tend - ntile
    pos = jnp.sum(oh * tstart[None, :], axis=1) * tm + rank
    n_valid = tend[-1]
    tid = jnp.arange(n_tiles, dtype=jnp.int32)
    tile_e = jnp.minimum(jnp.sum((tid[:, None] >= tend[None, :]).astype(jnp.int32), axis=1), n_exp - 1)
    rows = jnp.clip(cnt[tile_e] - (tid - tstart[tile_e]) * tm, 0, tm)
    rows = jnp.where(tid < n_valid, rows, 0)
    tile_e = jnp.where(tid < n_valid, tile_e, tile_e[jnp.maximum(n_valid - 1, 0)])
    tile_nsub = (rows + sub - 1) // sub
    nsub_max = tm // sub
    sub_valid = (jnp.arange(nsub_max, dtype=jnp.int32)[None, :] < tile_nsub[:, None]).astype(jnp.int32).reshape(-1)
    tok = jnp.arange(M, dtype=jnp.int32)
    src = jnp.zeros((n_tiles * tm,), jnp.int32).at[pos].set(jnp.concatenate([tok, tok]))
    return (pos[:M], pos[M:], src, sub_valid, tile_e.astype(jnp.int32), rows.astype(jnp.int32),
            n_valid.reshape(1).astype(jnp.int32))


def kernel(x_prompt, x_sample, cache_mem_k, cache_mem_v, state_conv, cache_win_k, cache_win_v, mem_prompt, norm_mix,
           norm_ffn, norm_mem, norm_kv, norm_final, w_mem_kv, w_in_a, w_dw, b_dw, ln_conv_g, ln_conv_b, w_out_a, w_kv,
           w_in_b, sinks, w_out_b, w_gu_dense, w_down_dense, w_router, w_gu_exp, w_down_exp):
    B, T, D = x_prompt.shape
    DB, DT, _ = x_sample.shape
    depth, _, MT, MH, MHD = cache_mem_k.shape
    C = state_conv.shape[-1]
    n_kv = cache_win_k.shape[2]
    KVW = n_kv * HD
    MW = MH * MHD
    n_exp = w_router.shape[-1]
    F = w_down_dense.shape[1]
    NP, NS = B * T, DB * DT
    M = NP + NS
    assert depth == 2 and MH == MEM_HEADS and C % MW == 0 and (8 % DT) == 0 and T % WINDOW == 0
    seqs_per_blk = 8 // DT
    sblk = seqs_per_blk * DT
    sbase = NP // sblk

    bm = _pick_tile(M, 2080, BF16_SUBLANES)
    bn = _pick_tile(MW, 512, LANES)

    mem2d = mem_prompt.reshape(B * MT, D)
    mem_kv = []
    for l in range(depth):
        (mn,) = _addnorm(mem2d, None, [norm_mem[l]], emit_sum=False)
        mem_kv.append(_mm(mn, w_mem_kv[l], bm=B * MT, bn=bn))
    mem_k_p = jnp.stack([kv[:, :MW].reshape(B, MT, MH, MHD) for kv in mem_kv])
    mem_v_p = jnp.stack([kv[:, MW:].reshape(B, MT, MH, MHD) for kv in mem_kv])
    cmk = cache_mem_k.reshape(depth, DB, MT, MW)
    cmv = cache_mem_v.reshape(depth, DB, MT, MW)

    tq_mem = _pick_tile(T, 512, 8)

    def mem_attention(z, qcol, l, cat):
        cat = _mem_attn(z, qcol, mem_kv[l], mem_kv[l],
                        (pl.BlockSpec((MT, MW), lambda b, n: (b, 0)), pl.BlockSpec((MT, MW), lambda b, n: (b, 1))),
                        cat, grid=(B, T // tq_mem), row_map=lambda b, n: b * (T // tq_mem) + n, nseq=1,
                        rows_per_seq=tq_mem, tq=tq_mem, mem_tokens=MT, W=MW)
        sspec = pl.BlockSpec((None, seqs_per_blk, MT, MW), lambda s: (l, s, 0, 0))
        return _mem_attn(z, qcol, cmk, cmv, (sspec, sspec), cat, grid=(DB // seqs_per_blk,),
                         row_map=lambda s: sbase + s, nseq=seqs_per_blk, rows_per_seq=DT, tq=sblk, mem_tokens=MT, W=MW)

    h0 = jnp.concatenate([x_prompt.reshape(NP, D), x_sample.reshape(NS, D)], axis=0)

    (hn,) = _addnorm(h0, None, [norm_mix[0]], emit_sum=False)
    z1 = _mm(hn, w_in_a[0], bm=bm, bn=bn)
    cat, tail = _conv_prompt(z1, w_dw[0], b_dw[0], ln_conv_g[0], ln_conv_b[0], batch=B, seq=T, C=C, cat_w=C + MW,
                             tt=WINDOW)
    cat, glu_s = _conv_sample(z1, state_conv[0], w_dw[0], b_dw[0], ln_conv_g[0], ln_conv_b[0], cat, n_prompt=NP,
                              dec_seq=DT, C=C)
    cat = mem_attention(z1, 2 * C // MW, 0, cat)
    y = _mm(cat, w_out_a[0], bm=bm, bn=bn)
    h1, hn = _addnorm(h0, y, [norm_ffn[0]], emit_sum=True)
    dense_tabs = (jnp.zeros((M // bm,), jnp.int32), jnp.full((M // bm,), bm, jnp.int32),
                  jnp.full((1,), M // bm, jnp.int32))
    act = _glu_up(hn, w_gu_dense, *dense_tabs, bm=bm, chunk=bm, gran=bm, bn=_pick_tile(F, 256, LANES))
    y = _down(act, w_down_dense, *dense_tabs, bm=bm, chunk=bm, gran=bm, bn=_pick_tile(D, 1024, LANES),
              bk=_pick_tile(F, 1024, LANES))
    h2, hn_kv, hn = _addnorm(h1, y, [norm_kv, norm_mix[1]], emit_sum=True)

    kv = _mm(hn_kv, w_kv, bm=bm, bn=bn)

    z2 = _mm(hn, w_in_b[0], bm=bm, bn=bn)
    nblk = T // WINDOW
    cat = jnp.zeros((M, C + MW), BF16)

    def prev_row(b, n):
        return jnp.maximum(b * nblk + n - 1, b * nblk)

    cat = _swa(sinks[0], z2, kv, kv, kv, cat, grid=(B, nblk), row_map=lambda b, n: b * nblk + n,
               prev_specs=(pl.BlockSpec((WINDOW, KVW), lambda b, n: (prev_row(b, n), 0)),
                           pl.BlockSpec((WINDOW, KVW), lambda b, n: (prev_row(b, n), 1))),
               nseq=1, tq=WINDOW, q_w=C, kv_w=KVW, prev_from_grid=True, pos0=0)
    pspec = pl.BlockSpec((seqs_per_blk, WINDOW, KVW), lambda s: (s, 0, 0))
    cat = _swa(sinks[0], z2, cache_win_k.reshape(DB, WINDOW, KVW), cache_win_v.reshape(DB, WINDOW, KVW), kv, cat,
               grid=(DB // seqs_per_blk,), row_map=lambda s: sbase + s, prev_specs=(pspec, pspec),
               nseq=seqs_per_blk, tq=DT, q_w=C, kv_w=KVW, prev_from_grid=False, pos0=PAST_LEN)
    cat = mem_attention(z2, C // MW, 1, cat)
    y = _mm(cat, w_out_b[0], bm=bm, bn=bn)
    h3, route = _router(h2, y, norm_ffn[1], w_router[0])

    mean_rows = TOP_K * M // n_exp
    gran = 128 if mean_rows >= 1024 else 16
    chunk = 4 * gran
    tm = -(-(mean_rows * 29 // 20) // chunk) * chunk
    n_tiles = TOP_K * M // tm + n_exp
    pos1, pos2, src, sub_valid, tile_e, tile_rows, n_valid = _moe_tables(route, n_exp, tm, gran, n_tiles)
    xs = _gather_norm(h3, norm_ffn[1], src, sub_valid, sub=gran)
    act = _glu_up(xs, w_gu_exp[0], tile_e, tile_rows, n_valid, bm=tm, chunk=chunk, gran=gran,
                  bn=_pick_tile(F, 256, LANES))
    ys = _down(act, w_down_exp[0], tile_e, tile_rows, n_valid, bm=tm, chunk=chunk, gran=gran,
               bn=_pick_tile(D, 1024, LANES), bk=_pick_tile(F, 1024, LANES))
    y_p, y_s = _combine(h3, ys, pos1, pos2, route[:, 2:3], route[:, 3:4], norm_final, n_prompt=NP,
                        tm=_pick_tile(NS, 128, 8))

    conv_p = tail[None, :, HALO - (CONV_W - 1):, :]
    conv_s = jnp.concatenate([state_conv[0][:, DT:], glu_s.reshape(DB, DT, C)], axis=1)[None]
    kv_p = kv[:NP].reshape(B, T, 2, n_kv, HD)[:, T - WINDOW:]
    kv_s = kv[NP:].reshape(DB, DT, 2, n_kv, HD)
    kp, vp = kv_p[:, :, 0], kv_p[:, :, 1]
    ks = jnp.concatenate([cache_win_k[:, DT:], kv_s[:, :, 0]], axis=1)
    vs = jnp.concatenate([cache_win_v[:, DT:], kv_s[:, :, 1]], axis=1)
    return (y_p.reshape(B, T, D), y_s.reshape(DB, DT, D), mem_k_p, mem_v_p, conv_p, kp, vp, conv_s, ks, vs)
```
